```python
import jax, jax.numpy as jnp
from jax import lax
import numpy as np

D_MODEL = 1024
BATCH = 4
SEQ = 8192
DEPTH = 2
DEC_BATCH = 2
DEC_SEQ = 8192
PAST_LEN = 128

GRID_W = 64
N_HEADS_A = 8
N_KV_HEADS_A = 2
HEAD_DIM_A = 64
GROUP_A = N_HEADS_A // N_KV_HEADS_A
D_A = N_HEADS_A * HEAD_DIM_A
D_KV_A = N_KV_HEADS_A * HEAD_DIM_A
ROPE_THETA = 10000.0
ROPE_PAIRS = HEAD_DIM_A // 4
Q_BLOCK = 128
N_HEADS_B = 4
HEAD_DIM_B = 128
D_B = N_HEADS_B * HEAD_DIM_B
CHUNK_B = 64
QK_CONV_B = 3
N_GATES_B = 4 * N_HEADS_B
IN_COLS = D_A + 2 * D_KV_A + 4 * D_B + N_GATES_B
D_C = D_MODEL
CONV_C = 31
D_FF = 2816
CONV_FF = 3
N_EVEN = (DEPTH + 1) // 2
N_ODD = DEPTH // 2
EPS = 1e-6

kernel_name = "hybrid_bidir_attn_mlstm_conformer_convffn"


def rmsnorm(x, g):
    xf = x.astype(jnp.float32)
    y = xf * lax.rsqrt(jnp.mean(xf * xf, axis=-1, keepdims=True) + EPS)
    return (y * g.astype(jnp.float32)).astype(x.dtype)


def rms_f32(x, g):
    xf = x.astype(jnp.float32)
    return xf * lax.rsqrt(jnp.mean(xf * xf, axis=-1, keepdims=True) + EPS) * g.astype(jnp.float32)


def layernorm(x, g, b):
    xf = x.astype(jnp.float32)
    mu = jnp.mean(xf, axis=-1, keepdims=True)
    xc = xf - mu
    y = xc * lax.rsqrt(jnp.mean(xc * xc, axis=-1, keepdims=True) + EPS)
    return (y * g.astype(jnp.float32) + b.astype(jnp.float32)).astype(x.dtype)


def depthwise_conv(x, w):
    k, c = w.shape
    return lax.conv_general_dilated(
        x, w[:, None, :].astype(x.dtype), window_strides=(1,),
        padding=[(k // 2, k // 2)], dimension_numbers=("NWC", "WIO", "NWC"),
        feature_group_count=c)


def axial_rope(n):
    rows = n // GRID_W
    row_idx = jnp.repeat(jnp.arange(rows, dtype=jnp.float32), GRID_W)
    col_idx = jnp.tile(jnp.arange(GRID_W, dtype=jnp.float32), rows)
    inv = ROPE_THETA ** (-jnp.arange(ROPE_PAIRS, dtype=jnp.float32) / ROPE_PAIRS)
    ang = jnp.stack([row_idx[:, None] * inv, col_idx[:, None] * inv], axis=1)
    return jnp.cos(ang), jnp.sin(ang)


def apply_rope(x, cos, sin):
    shape = x.shape
    n = shape[1]
    xs = x.reshape(shape[:-1] + (2, 2, ROPE_PAIRS))
    x1, x2 = xs[..., 0, :], xs[..., 1, :]
    bshape = (n,) + (1,) * (x.ndim - 3) + (2, ROPE_PAIRS)
    c, s = cos.reshape(bshape), sin.reshape(bshape)
    out = jnp.stack([x1 * c - x2 * s, x2 * c + x1 * s], axis=-2)
    return out.reshape(shape)


def block_attention(q, k, v):
    bsz, n = q.shape[0], q.shape[1]
    nb = n // Q_BLOCK
    qb = q.reshape(bsz, nb, Q_BLOCK, N_KV_HEADS_A, GROUP_A, HEAD_DIM_A).transpose(1, 0, 2, 3, 4, 5)
    scale = HEAD_DIM_A ** -0.5

    def one_block(qi):
        s = jnp.einsum("bqhgd,bkhd->bhgqk", qi, k, preferred_element_type=jnp.float32) * scale
        p = jax.nn.softmax(s, axis=-1).astype(v.dtype)
        return jnp.einsum("bhgqk,bkhd->bqhgd", p, v)

    o = lax.map(one_block, qb)
    return o.transpose(1, 0, 2, 3, 4, 5).reshape(bsz, n, D_A)


def mlstm_scan(q, k, v, i_pre, f_pre):
    bsz, nh, n, dh = q.shape
    nc = n // CHUNK_B
    logf = jax.nn.log_sigmoid(f_pre)

    def chunks(t):
        t = t.reshape((bsz, nh, nc, CHUNK_B) + t.shape[3:])
        return jnp.moveaxis(t, 2, 0)

    qc, kc, vc, ic = chunks(q), chunks(k), chunks(v), chunks(i_pre)
    bc = jnp.cumsum(chunks(logf), axis=-1)
    tril = jnp.tril(jnp.ones((CHUNK_B, CHUNK_B), dtype=bool))

    def step(carry, inp):
        c_mat, n_vec, m = carry
        qj, kj, vj, ij, bj = inp
        log_d = bj[..., :, None] - bj[..., None, :] + ij[..., None, :]
        log_d = jnp.where(tril, log_d, -jnp.inf)
        inter = bj + m[..., None]
        m_row = jnp.maximum(jnp.max(log_d, axis=-1), inter)
        s = jnp.einsum("bhld,bhkd->bhlk", qj, kj) * jnp.exp(log_d - m_row[..., None])
        w_inter = jnp.exp(inter - m_row)
        num = (w_inter[..., None] * jnp.einsum("bhvk,bhlk->bhlv", c_mat, qj)
               + jnp.einsum("bhlk,bhkv->bhlv", s, vj))
        den = w_inter * jnp.einsum("bhk,bhlk->bhl", n_vec, qj) + jnp.sum(s, axis=-1)
        h = num / jnp.maximum(jnp.abs(den), jnp.exp(-m_row))[..., None]
        b_last = bj[..., -1]
        w_log = b_last[..., None] - bj + ij
        m_new = jnp.maximum(b_last + m, jnp.max(w_log, axis=-1))
        w = jnp.exp(w_log - m_new[..., None])
        decay = jnp.exp(b_last + m - m_new)
        c_new = decay[..., None, None] * c_mat + jnp.einsum("bhl,bhlv,bhlk->bhvk", w, vj, kj)
        n_new = decay[..., None] * n_vec + jnp.einsum("bhl,bhlk->bhk", w, kj)
        return (c_new, n_new, m_new), h

    init = (jnp.zeros((bsz, nh, dh, dh), jnp.float32),
            jnp.zeros((bsz, nh, dh), jnp.float32),
            jnp.zeros((bsz, nh), jnp.float32))
    _, h = lax.scan(step, init, (qc, kc, vc, ic, bc))
    return jnp.moveaxis(h, 0, 2).reshape(bsz, nh, n, dh)


def even_mixer(h, w_in, q_gain, k_gain, w_qk_conv, b_gates, h_gain, w_out):
    bsz, n, _ = h.shape
    proj = h @ w_in
    sizes = (D_A, D_KV_A, D_KV_A, D_B, D_B, D_B, D_B, N_GATES_B)
    idx = np.cumsum(sizes)[:-1].tolist()
    qa, ka, va, qb, kb, vb, ob, gates = jnp.split(proj, idx, axis=-1)

    cos, sin = axial_rope(n)
    qa = apply_rope(rms_f32(qa.reshape(bsz, n, N_KV_HEADS_A, GROUP_A, HEAD_DIM_A), q_gain), cos, sin).astype(h.dtype)
    ka = apply_rope(rms_f32(ka.reshape(bsz, n, N_KV_HEADS_A, HEAD_DIM_A), k_gain), cos, sin).astype(h.dtype)
    va = va.reshape(bsz, n, N_KV_HEADS_A, HEAD_DIM_A)
    out_a = block_attention(qa, ka, va)

    qk = jax.nn.silu(depthwise_conv(jnp.concatenate([qb, kb], axis=-1), w_qk_conv))
    qb, kb = jnp.split(qk, 2, axis=-1)

    def to_heads(t):
        return t.astype(jnp.float32).reshape(bsz, n, N_HEADS_B, HEAD_DIM_B).transpose(0, 2, 1, 3)

    q_h, k_h, v_h = to_heads(qb), to_heads(kb) * (HEAD_DIM_B ** -0.5), to_heads(vb)
    g = (gates.astype(jnp.float32) + b_gates.astype(jnp.float32))
    g = g.reshape(bsz, n, 4, N_HEADS_B).transpose(2, 0, 3, 1)

    def flip(t):
        return jnp.flip(t, axis=2)

    h_fwd = mlstm_scan(q_h, k_h, v_h, g[0], g[1])
    h_bwd = flip(mlstm_scan(flip(q_h), flip(k_h), flip(v_h), flip(g[2]), flip(g[3])))
    hb = (h_fwd + h_bwd).transpose(0, 2, 1, 3)
    hb = hb * lax.rsqrt(jnp.mean(hb * hb, axis=-1, keepdims=True) + EPS)
    hb = hb.reshape(bsz, n, D_B) * h_gain.astype(jnp.float32)
    out_b = (hb * jax.nn.sigmoid(ob.astype(jnp.float32))).astype(h.dtype)

    return jnp.concatenate([out_a, out_b], axis=-1) @ w_out


def conformer_conv(h, w_pw1, b_pw1, w_dw, b_dw, ln_g, ln_b, w_pw2, b_pw2):
    u = h @ w_pw1 + b_pw1
    a, gate = jnp.split(u, 2, axis=-1)
    u = a * jax.nn.sigmoid(gate)
    u = depthwise_conv(u, w_dw) + b_dw
    u = jax.nn.silu(layernorm(u, ln_g, ln_b))
    return u @ w_pw2 + b_pw2


def conv_ffn(h, w_up, w_dw, b_dw, w_down):
    u = depthwise_conv(h @ w_up, w_dw) + b_dw
    gate, val = jnp.split(u, 2, axis=-1)
    return (jax.nn.silu(gate) * val) @ w_down


def trunk(x, mix_norm_e, w_in, q_gain_a, k_gain_a, w_qk_conv_b, b_gates_b, h_gain_b, w_out_e,
          mix_norm_o, w_pw1_c, b_pw1_c, w_dw_c, b_dw_c, ln_g_c, ln_b_c, w_pw2_c, b_pw2_c,
          ffn_norm, w_up, w_dw_ff, b_dw_ff, w_down):
    for layer in range(DEPTH):
        j = layer // 2
        if layer % 2 == 0:
            x = x + even_mixer(rmsnorm(x, mix_norm_e[j]), w_in[j], q_gain_a[j], k_gain_a[j],
                               w_qk_conv_b[j], b_gates_b[j], h_gain_b[j], w_out_e[j])
        else:
            x = x + conformer_conv(rmsnorm(x, mix_norm_o[j]), w_pw1_c[j], b_pw1_c[j], w_dw_c[j],
                                   b_dw_c[j], ln_g_c[j], ln_b_c[j], w_pw2_c[j], b_pw2_c[j])
        x = x + conv_ffn(rmsnorm(x, ffn_norm[layer]), w_up[layer], w_dw_ff[layer],
                         b_dw_ff[layer], w_down[layer])
    return x


def setup_inputs(seed: int = 0) -> dict:
    key = jax.random.key(seed)
    ks = jax.random.split(key, 32)
    f32 = jnp.float32

    def nrm(k, shape, scale):
        return jax.random.normal(k, shape, f32) * scale

    def gain(k, shape):
        return 1.0 + nrm(k, shape, 0.02)

    f_bias = jnp.linspace(3.0, 6.0, N_HEADS_B, dtype=f32)
    zeros_h = jnp.zeros((N_HEADS_B,), f32)
    gate_base = jnp.stack([zeros_h, f_bias, zeros_h, f_bias], axis=0)
    b_gates_b = (gate_base[None] + nrm(ks[7], (N_EVEN, 4, N_HEADS_B), 0.1)).reshape(N_EVEN, N_GATES_B)

    return {
        "x_prompt": nrm(ks[0], (BATCH, SEQ, D_MODEL), 1.0),
        "x_sample": nrm(ks[1], (DEC_BATCH, DEC_SEQ, D_MODEL), 1.0),
        "mix_norm_e": gain(ks[2], (N_EVEN, D_MODEL)),
        "w_in": nrm(ks[3], (N_EVEN, D_MODEL, IN_COLS), D_MODEL ** -0.5),
        "q_gain_a": gain(ks[4], (N_EVEN, HEAD_DIM_A)),
        "k_gain_a": gain(ks[5], (N_EVEN, HEAD_DIM_A)),
        "w_qk_conv_b": nrm(ks[6], (N_EVEN, QK_CONV_B, 2 * D_B), QK_CONV_B ** -0.5),
        "b_gates_b": b_gates_b,
        "h_gain_b": gain(ks[8], (N_EVEN, D_B)),
        "w_out_e": nrm(ks[9], (N_EVEN, D_A + D_B, D_MODEL), (D_A + D_B) ** -0.5),
        "mix_norm_o": gain(ks[10], (N_ODD, D_MODEL)),
        "w_pw1_c": nrm(ks[11], (N_ODD, D_MODEL, 2 * D_C), D_MODEL ** -0.5),
        "b_pw1_c": nrm(ks[12], (N_ODD, 2 * D_C), 0.02),
        "w_dw_c": nrm(ks[13], (N_ODD, CONV_C, D_C), CONV_C ** -0.5),
        "b_dw_c": nrm(ks[14], (N_ODD, D_C), 0.02),
        "ln_g_c": gain(ks[15], (N_ODD, D_C)),
        "ln_b_c": nrm(ks[16], (N_ODD, D_C), 0.02),
        "w_pw2_c": nrm(ks[17], (N_ODD, D_C, D_MODEL), D_C ** -0.5),
        "b_pw2_c": nrm(ks[18], (N_ODD, D_MODEL), 0.02),
        "ffn_norm": gain(ks[19], (DEPTH, D_MODEL)),
        "w_up": nrm(ks[20], (DEPTH, D_MODEL, 2 * D_FF), D_MODEL ** -0.5),
        "w_dw_ff": nrm(ks[21], (DEPTH, CONV_FF, 2 * D_FF), CONV_FF ** -0.5),
        "b_dw_ff": nrm(ks[22], (DEPTH, 2 * D_FF), 0.02),
        "w_down": nrm(ks[23], (DEPTH, D_FF, D_MODEL), D_FF ** -0.5),
    }


def reference(x_prompt, x_sample, mix_norm_e, w_in, q_gain_a, k_gain_a, w_qk_conv_b, b_gates_b,
              h_gain_b, w_out_e, mix_norm_o, w_pw1_c, b_pw1_c, w_dw_c, b_dw_c, ln_g_c, ln_b_c,
              w_pw2_c, b_pw2_c, ffn_norm, w_up, w_dw_ff, b_dw_ff, w_down):
    weights = (mix_norm_e, w_in, q_gain_a, k_gain_a, w_qk_conv_b, b_gates_b, h_gain_b, w_out_e,
               mix_norm_o, w_pw1_c, b_pw1_c, w_dw_c, b_dw_c, ln_g_c, ln_b_c, w_pw2_c, b_pw2_c,
               ffn_norm, w_up, w_dw_ff, b_dw_ff, w_down)
    y_prompt = trunk(x_prompt, *weights)
    y_sample = trunk(x_sample, *weights)
    return (y_prompt, y_sample)
```

```python
import functools

import numpy as np
import jax
import jax.numpy as jnp
from jax import lax
from jax.experimental import pallas as pl
from jax.experimental.pallas import tpu as pltpu

D_MODEL = 1024
GRID_W = 64
N_HEADS_A = 8
N_KV_HEADS_A = 2
HEAD_DIM_A = 64
D_A = N_HEADS_A * HEAD_DIM_A
D_KV_A = N_KV_HEADS_A * HEAD_DIM_A
ROPE_THETA = 10000.0
ROPE_PAIRS = HEAD_DIM_A // 4
N_HEADS_B = 4
HEAD_DIM_B = 128
D_B = N_HEADS_B * HEAD_DIM_B
CHUNK_B = 64
N_GATES_B = 4 * N_HEADS_B
IN_COLS = D_A + 2 * D_KV_A + 4 * D_B + N_GATES_B
IN_COLS_PAD = 2944
CONV_C = 31
D_FF = 2816
EPS = 1e-6

LANES = 128
HALO = 16
VMEM_LIMIT = 56 * 1024 * 1024

F32 = jnp.float32
BF16 = jnp.bfloat16

_O_QA, _O_KA, _O_VA, _O_QKB, _O_VB, _O_OB, _O_G = 0, 512, 640, 768, 1792, 2304, 2816


def _params(n_axes):
    return pltpu.CompilerParams(dimension_semantics=("arbitrary",) * n_axes,
                                vmem_limit_bytes=VMEM_LIMIT)


def _dot(a, b):
    return jnp.dot(a, b, preferred_element_type=F32)


def _rms(x, gain):
    ms = jnp.mean(x * x, axis=-1, keepdims=True)
    return x * lax.rsqrt(ms + EPS) * gain


def _sigmoid(x):
    return 1.0 / (1.0 + jnp.exp(-x))


def _normed_window(xp_ref, x_ref, xn_ref, gain, first, last):
    hp = jnp.where(first, 0.0, _rms(xp_ref[...], gain))
    hn = jnp.where(last, 0.0, _rms(xn_ref[...], gain))
    h = _rms(x_ref[...], gain)
    return jnp.concatenate([hp, h, hn], axis=0).astype(BF16)


def _conv3(u, cw, tm):
    n = u.shape[0]
    um = pltpu.roll(u, 1, 0)[HALO:HALO + tm]
    uc = u[HALO:HALO + tm]
    up = pltpu.roll(u, n - 1, 0)[HALO:HALO + tm]
    return um * cw[0:1] + uc * cw[1:2] + up * cw[2:3]


def _halo_specs(tm, d, n_tokens, axis=0, n_axes=1):
    r = tm // HALO
    last_blk = n_tokens // HALO - 1

    def pick(idx):
        return idx[axis]

    prev = pl.BlockSpec((HALO, d), lambda *idx: (jnp.maximum(pick(idx) * r - 1, 0), 0))
    main = pl.BlockSpec((tm, d), lambda *idx: (pick(idx), 0))
    nxt = pl.BlockSpec((HALO, d), lambda *idx: (jnp.minimum((pick(idx) + 1) * r, last_blk), 0))
    return prev, main, nxt


def _rope(xn, cos, sin, width):
    lane = lax.broadcasted_iota(jnp.int32, xn.shape, 1)
    first_half = (lane % (2 * ROPE_PAIRS)) < ROPE_PAIRS
    partner = jnp.where(first_half, pltpu.roll(xn, width - ROPE_PAIRS, 1), pltpu.roll(xn, ROPE_PAIRS, 1))
    return xn * cos + partner * sin


def _in_proj_kernel(xp_ref, x_ref, xn_ref, gain_ref, w_ref, cos_ref, sin_ref, segq_ref, segk_ref,
                    qg_ref, kg_ref, cw_ref,
                    q_out, kt_out, v_out, qb_out, kb_out, vb_out, ob_out, g_out, *, tiles_per_seq):
    i = pl.program_id(0)
    tm = x_ref.shape[0]
    first = (i % tiles_per_seq) == 0
    last = (i % tiles_per_seq) == tiles_per_seq - 1
    hext = _normed_window(xp_ref, x_ref, xn_ref, gain_ref[...], first, last)
    h = hext[HALO:HALO + tm]

    u = _dot(hext, w_ref[:, _O_QKB:_O_VB])
    c = _conv3(u, cw_ref[...], tm)
    act = c * _sigmoid(c)
    qb_out[...] = act[:, :D_B].astype(BF16)
    kb_out[...] = (act[:, D_B:] * (HEAD_DIM_B ** -0.5)).astype(BF16)
    vb_out[...] = _dot(h, w_ref[:, _O_VB:_O_OB]).astype(BF16)
    ob_out[...] = _dot(h, w_ref[:, _O_OB:_O_G])
    g_out[...] = _dot(h, w_ref[:, _O_G:IN_COLS_PAD])

    cos2 = cos_ref[...]
    sin2 = sin_ref[...]
    qa = _dot(h, w_ref[:, _O_QA:_O_KA])
    ms = _dot((qa * qa).astype(BF16), segq_ref[...])
    qn = qa * lax.rsqrt(ms + EPS) * qg_ref[...]
    cos = jnp.concatenate([cos2] * (D_A // LANES), axis=1)
    sin = jnp.concatenate([sin2] * (D_A // LANES), axis=1)
    q_out[...] = (_rope(qn, cos, sin, D_A) * (HEAD_DIM_A ** -0.5)).astype(BF16)

    ka = _dot(h, w_ref[:, _O_KA:_O_VA])
    msk = _dot((ka * ka).astype(BF16), segk_ref[...])
    kn = ka * lax.rsqrt(msk + EPS) * kg_ref[...]
    kt = _rope(kn, cos2, sin2, D_KV_A).T
    k0, k1 = kt[:HEAD_DIM_A], kt[HEAD_DIM_A:]
    kt_out[0] = jnp.concatenate([k0, k0, k1, k1], axis=0).astype(BF16)

    va = _dot(h, w_ref[:, _O_VA:_O_QKB])
    vr = pltpu.roll(va, HEAD_DIM_A, 1)
    left = lax.broadcasted_iota(jnp.int32, va.shape, 1) < HEAD_DIM_A
    zero = jnp.zeros_like(va)
    v_out[...] = jnp.concatenate([jnp.where(left, va, zero), jnp.where(left, zero, vr),
                                  jnp.where(left, vr, zero), jnp.where(left, zero, va)], axis=1).astype(BF16)


def _in_proj(x2d, seq, tm, gain, w_pad, cos, sin, segq, segk, qg, kg, cw):
    n_tok = x2d.shape[0]
    nt = n_tok // tm
    tps = seq // tm
    prev, main, nxt = _halo_specs(tm, D_MODEL, n_tok)
    const = lambda shape: pl.BlockSpec(shape, lambda i: (0,) * len(shape))
    rope_spec = pl.BlockSpec((tm, LANES), lambda i: (i % tps, 0))
    tok = lambda d: pl.BlockSpec((tm, d), lambda i: (i, 0))
    out_shape = (
        jax.ShapeDtypeStruct((n_tok, D_A), BF16),
        jax.ShapeDtypeStruct((nt, 2 * LANES, tm), BF16),
        jax.ShapeDtypeStruct((n_tok, 4 * LANES), BF16),
        jax.ShapeDtypeStruct((n_tok, D_B), BF16),
        jax.ShapeDtypeStruct((n_tok, D_B), BF16),
        jax.ShapeDtypeStruct((n_tok, D_B), BF16),
        jax.ShapeDtypeStruct((n_tok, D_B), F32),
        jax.ShapeDtypeStruct((n_tok, LANES), F32),
    )
    out_specs = (tok(D_A), pl.BlockSpec((1, 2 * LANES, tm), lambda i: (i, 0, 0)), tok(4 * LANES),
                 tok(D_B), tok(D_B), tok(D_B), tok(D_B), tok(LANES))
    return pl.pallas_call(
        functools.partial(_in_proj_kernel, tiles_per_seq=tps),
        grid=(nt,),
        in_specs=[prev, main, nxt, const((1, D_MODEL)), const((D_MODEL, IN_COLS_PAD)), rope_spec, rope_spec,
                  const((D_A, D_A)), const((D_KV_A, D_KV_A)), const((1, D_A)), const((1, D_KV_A)),
                  const((3, 2 * D_B))],
        out_specs=out_specs,
        out_shape=out_shape,
        compiler_params=_params(1),
        name="in_proj",
    )(x2d, x2d, x2d, gain, w_pad, cos, sin, segq, segk, qg, kg, cw)


def _attn_kernel(q_ref, kt_ref, v_ref, o_ref, *, n_chunks, kc):
    tq = q_ref.shape[0]
    left = lax.broadcasted_iota(jnp.int32, (tq, LANES), 1) < HEAD_DIM_A
    for pair in range(2):
        qp = q_ref[:, pair * LANES:(pair + 1) * LANES].astype(F32)
        o_pair = None
        for side in range(2):
            qh = jnp.where(left if side == 0 else jnp.logical_not(left), qp, 0.0).astype(BF16)

            def body(c, carry, qh=qh, side=side):
                m, l, acc = carry
                s = _dot(qh, kt_ref[c])
                m_new = jnp.maximum(m, jnp.max(s, axis=-1, keepdims=True))
                p = jnp.exp(s - m_new)
                alpha = jnp.exp(m - m_new)
                l = alpha * l + jnp.sum(p, axis=-1, keepdims=True)
                vv = v_ref[pl.ds(pl.multiple_of(c * kc, kc), kc), side * LANES:(side + 1) * LANES]
                acc = alpha * acc + _dot(p.astype(BF16), vv)
                return m_new, l, acc

            init = (jnp.full((tq, 1), -jnp.inf, F32), jnp.zeros((tq, 1), F32), jnp.zeros((tq, LANES), F32))
            _, l, acc = lax.fori_loop(0, n_chunks, body, init)
            o = acc / l
            o_pair = o if o_pair is None else o_pair + o
        o_ref[:, pair * LANES:(pair + 1) * LANES] = o_pair.astype(BF16)


def _attention(q, kt, v, batch, seq, tq):
    n_tok = q.shape[0]
    kc = kt.shape[2]
    nq = seq // tq
    n_chunks = seq // kc
    return pl.pallas_call(
        functools.partial(_attn_kernel, n_chunks=n_chunks, kc=kc),
        grid=(batch, N_KV_HEADS_A, nq),
        in_specs=[pl.BlockSpec((tq, 2 * LANES), lambda b, g, i: (b * nq + i, g)),
                  pl.BlockSpec((n_chunks, LANES, kc), lambda b, g, i: (b, g, 0)),
                  pl.BlockSpec((seq, 2 * LANES), lambda b, g, i: (b, g))],
        out_specs=pl.BlockSpec((tq, 2 * LANES), lambda b, g, i: (b * nq + i, g)),
        out_shape=jax.ShapeDtypeStruct((n_tok, D_A), BF16),
        compiler_params=_params(3),
        name="attention",
    )(q, kt, v)


def _log_sigmoid(x):
    return jnp.minimum(x, 0.0) - jnp.log1p(jnp.exp(-jnp.abs(x)))


def _split3(x):
    hi = x.astype(BF16)
    r1 = x - hi.astype(F32)
    mid = r1.astype(BF16)
    lo = (r1 - mid.astype(F32)).astype(BF16)
    return hi, mid, lo


def _mlstm_kernel(qf_ref, kf_ref, vf_ref, gcf_ref, grf_ref, qb_ref, kb_ref, vb_ref, gcb_ref, grb_ref,
                  brow_ref, bcol_ref, hf_out, hb_out, ct_scr, m_scr, *, cps):
    j = pl.program_id(1)
    L = CHUNK_B
    dh = HEAD_DIM_B

    @pl.when(j == 0)
    def _():
        ct_scr[...] = jnp.zeros_like(ct_scr)
        m_scr[...] = jnp.zeros_like(m_scr)

    row = lax.broadcasted_iota(jnp.int32, (L, L), 0)
    col = lax.broadcasted_iota(jnp.int32, (L, L), 1)
    lower = col <= row
    upper = col >= row
    linc = jnp.where(lower, 1.0, 0.0).astype(BF16)
    uinc = jnp.where(upper, 1.0, 0.0).astype(BF16)
    ones_col = jnp.where(lax.broadcasted_iota(jnp.int32, (L, dh), 1) == 0, 1.0, 0.0).astype(BF16)

    dirs = ((qf_ref, kf_ref, vf_ref, gcf_ref, grf_ref, hf_out), (qb_ref, kb_ref, vb_ref, gcb_ref, grb_ref, hb_out))
    for cc in range(cps):
        for d, (q_ref, k_ref, v_ref, gc_ref, gr_ref, out_ref) in enumerate(dirs):
            ci = cc if d == 0 else cps - 1 - cc
            gcol = gc_ref[0, ci] + brow_ref[...]
            grow = gr_ref[0, ci] + bcol_ref[...]
            lf_c = _log_sigmoid(gcol)
            lf_r = _log_sigmoid(grow)
            tri_c = linc if d == 0 else uinc
            tri_r = uinc if d == 0 else linc
            b_c_all = sum(_dot(tri_c, piece) for piece in _split3(lf_c))
            b_r_all = sum(_dot(piece, tri_r) for piece in _split3(lf_r))
            mask = lower if d == 0 else upper
            rows = slice(ci * L, (ci + 1) * L)
            for hd in range(N_HEADS_B):
                unit = d * N_HEADS_B + hd
                ci_col = d * 2 * N_HEADS_B + hd
                cf_col = ci_col + N_HEADS_B
                lanes = slice(hd * dh, (hd + 1) * dh)
                i_c = gcol[:, ci_col:ci_col + 1]
                b_c = b_c_all[:, cf_col:cf_col + 1]
                i_r = grow[ci_col:ci_col + 1, :]
                b_r = b_r_all[cf_col:cf_col + 1, :]
                m = m_scr[unit][:, 0:1]
                q = q_ref[rows, lanes]
                k = k_ref[rows, lanes]
                v = v_ref[rows, lanes]

                log_d = jnp.where(mask, b_c - b_r + i_r, -jnp.inf)
                inter = b_c + m
                m_row = jnp.maximum(jnp.max(log_d, axis=-1, keepdims=True), inter)
                s = lax.dot_general(q, k, (((1,), (1,)), ((), ())), preferred_element_type=F32)
                s = s * jnp.exp(log_d - m_row)
                w_inter = jnp.exp(inter - m_row)
                ct = ct_scr[unit]
                v_aug = jnp.concatenate([v, ones_col], axis=1)
                tot = w_inter * _dot(q, ct.astype(BF16)) + _dot(s.astype(BF16), v_aug)
                num = tot[:, :dh]
                den = tot[:, dh:dh + 1]
                out_ref[rows, lanes] = num / jnp.maximum(jnp.abs(den), jnp.exp(-m_row))

                b_last = b_c[L - 1:L] if d == 0 else b_c[0:1]
                w_log = b_last - b_c + i_c
                m_new = jnp.maximum(b_last + m, jnp.max(w_log, axis=0, keepdims=True))
                w = jnp.exp(w_log - m_new)
                decay = jnp.exp(b_last + m - m_new)
                wv = (w * v_aug.astype(F32)).astype(BF16)
                upd = lax.dot_general(k, wv, (((0,), (0,)), ((), ())), preferred_element_type=F32)
                ct_scr[unit] = decay * ct + upd
                m_scr[unit] = jnp.broadcast_to(m_new, (1, LANES))


def _mlstm(qb, kb, vb, gcol, grow, brow, bcol, batch, seq, cps):
    n_tok = qb.shape[0]
    rows = cps * CHUNK_B
    ns = seq // rows
    fwd = lambda b, j: (b * ns + j, 0)
    bwd = lambda b, j: (b * ns + ns - 1 - j, 0)
    gfwd = lambda b, j: (b, j, 0, 0)
    gbwd = lambda b, j: (b, ns - 1 - j, 0, 0)
    tok = lambda im: pl.BlockSpec((rows, D_B), im)
    gc = lambda im: pl.BlockSpec((1, cps, CHUNK_B, LANES), im)
    gr = lambda im: pl.BlockSpec((1, cps, N_GATES_B, CHUNK_B), im)
    n_units = 2 * N_HEADS_B
    return pl.pallas_call(
        functools.partial(_mlstm_kernel, cps=cps),
        grid=(batch, ns),
        in_specs=[tok(fwd), tok(fwd), tok(fwd), gc(gfwd), gr(gfwd),
                  tok(bwd), tok(bwd), tok(bwd), gc(gbwd), gr(gbwd),
                  pl.BlockSpec((1, LANES), lambda b, j: (0, 0)),
                  pl.BlockSpec((N_GATES_B, 1), lambda b, j: (0, 0))],
        out_specs=(tok(fwd), tok(bwd)),
        out_shape=(jax.ShapeDtypeStruct((n_tok, D_B), F32), jax.ShapeDtypeStruct((n_tok, D_B), F32)),
        scratch_shapes=[pltpu.VMEM((n_units, HEAD_DIM_B, 2 * HEAD_DIM_B), F32),
                        pltpu.VMEM((n_units, 1, LANES), F32)],
        compiler_params=_params(2),
        name="mlstm",
    )(qb, kb, vb, gcol, grow, qb, kb, vb, gcol, grow, brow, bcol)


def _out_proj_kernel(x_ref, oa_ref, hf_ref, hb_ref, ob_ref, hg_ref, w_ref, o_ref):
    hsum = hf_ref[...] + hb_ref[...]
    parts = []
    for hd in range(N_HEADS_B):
        hh = hsum[:, hd * HEAD_DIM_B:(hd + 1) * HEAD_DIM_B]
        parts.append(hh * lax.rsqrt(jnp.mean(hh * hh, axis=-1, keepdims=True) + EPS))
    hn = jnp.concatenate(parts, axis=1) * hg_ref[...]
    out_b = (hn * _sigmoid(ob_ref[...])).astype(BF16)
    o_ref[...] = x_ref[...] + _dot(oa_ref[...], w_ref[:D_A]) + _dot(out_b, w_ref[D_A:])


def _out_proj(x2d, out_a, hf, hb, ob, hg, w, tm):
    n_tok = x2d.shape[0]
    tok = lambda d: pl.BlockSpec((tm, d), lambda i: (i, 0))
    return pl.pallas_call(
        _out_proj_kernel,
        grid=(n_tok // tm,),
        in_specs=[tok(D_MODEL), tok(D_A), tok(D_B), tok(D_B), tok(D_B),
                  pl.BlockSpec((1, D_B), lambda i: (0, 0)),
                  pl.BlockSpec((D_A + D_B, D_MODEL), lambda i: (0, 0))],
        out_specs=tok(D_MODEL),
        out_shape=jax.ShapeDtypeStruct((n_tok, D_MODEL), F32),
        compiler_params=_params(1),
        name="out_proj",
    )(x2d, out_a, hf, hb, ob, hg, w)


def _ffn_up_kernel(xp_ref, x_ref, xn_ref, gain_ref, wg_ref, wv_ref, cwg_ref, cwv_ref, cbg_ref, cbv_ref, o_ref,
                   *, tiles_per_seq):
    i = pl.program_id(1)
    tm = x_ref.shape[0]
    first = (i % tiles_per_seq) == 0
    last = (i % tiles_per_seq) == tiles_per_seq - 1
    hext = _normed_window(xp_ref, x_ref, xn_ref, gain_ref[...], first, last)
    gate = _conv3(_dot(hext, wg_ref[...]), cwg_ref[...], tm) + cbg_ref[...]
    val = _conv3(_dot(hext, wv_ref[...]), cwv_ref[...], tm) + cbv_ref[...]
    o_ref[...] = (gate * _sigmoid(gate) * val).astype(BF16)


def _ffn_up(x2d, seq, tm, tn, gain, w_up, cw, cb):
    n_tok = x2d.shape[0]
    nj = D_FF // tn
    prev, main, nxt = _halo_specs(tm, D_MODEL, n_tok, axis=1)
    col = lambda r, off: pl.BlockSpec((r, tn), lambda j, i: (0, off + j))
    return pl.pallas_call(
        functools.partial(_ffn_up_kernel, tiles_per_seq=seq // tm),
        grid=(nj, n_tok // tm),
        in_specs=[prev, main, nxt, pl.BlockSpec((1, D_MODEL), lambda j, i: (0, 0)),
                  col(D_MODEL, 0), col(D_MODEL, nj), col(3, 0), col(3, nj), col(1, 0), col(1, nj)],
        out_specs=pl.BlockSpec((tm, tn), lambda j, i: (i, j)),
        out_shape=jax.ShapeDtypeStruct((n_tok, D_FF), BF16),
        compiler_params=_params(2),
        name="ffn_up",
    )(x2d, x2d, x2d, gain, w_up, w_up, cw, cw, cb, cb)


def _ffn_down_kernel(x_ref, g_ref, w_ref, o_ref):
    o_ref[...] = x_ref[...] + _dot(g_ref[...], w_ref[...])


def _ffn_down(x2d, g, w, tm):
    n_tok = x2d.shape[0]
    return pl.pallas_call(
        _ffn_down_kernel,
        grid=(n_tok // tm,),
        in_specs=[pl.BlockSpec((tm, D_MODEL), lambda i: (i, 0)), pl.BlockSpec((tm, D_FF), lambda i: (i, 0)),
                  pl.BlockSpec((D_FF, D_MODEL), lambda i: (0, 0))],
        out_specs=pl.BlockSpec((tm, D_MODEL), lambda i: (i, 0)),
        out_shape=jax.ShapeDtypeStruct((n_tok, D_MODEL), F32),
        compiler_params=_params(1),
        name="ffn_down",
    )(x2d, g, w)


def _conf_glu_kernel(x_ref, gain_ref, w_ref, b_ref, o_ref):
    h = _rms(x_ref[...], gain_ref[...]).astype(BF16)
    u = _dot(h, w_ref[...]) + b_ref[...]
    o_ref[...] = u[:, :D_MODEL] * _sigmoid(u[:, D_MODEL:])


def _conf_glu(x2d, gain, w, b, tm):
    n_tok = x2d.shape[0]
    return pl.pallas_call(
        _conf_glu_kernel,
        grid=(n_tok // tm,),
        in_specs=[pl.BlockSpec((tm, D_MODEL), lambda i: (i, 0)), pl.BlockSpec((1, D_MODEL), lambda i: (0, 0)),
                  pl.BlockSpec((D_MODEL, 2 * D_MODEL), lambda i: (0, 0)),
                  pl.BlockSpec((1, 2 * D_MODEL), lambda i: (0, 0))],
        out_specs=pl.BlockSpec((tm, D_MODEL), lambda i: (i, 0)),
        out_shape=jax.ShapeDtypeStruct((n_tok, D_MODEL), F32),
        compiler_params=_params(1),
        name="conf_glu",
    )(x2d, gain, w, b)


CONV_ROWS = 8


def _conf_conv_kernel(up_ref, u_ref, un_ref, x_ref, wdw_ref, bdw_ref, lng_ref, lnb_ref, w2_ref, b2_ref, o_ref,
                      rot_scr, act_scr, *, tiles_per_seq):
    i = pl.program_id(0)
    tm = u_ref.shape[0]
    n = tm + 2 * HALO
    first = (i % tiles_per_seq) == 0
    last = (i % tiles_per_seq) == tiles_per_seq - 1
    win = jnp.concatenate([jnp.where(first, 0.0, up_ref[...]), u_ref[...], jnp.where(last, 0.0, un_ref[...])], axis=0)
    rot_scr[0] = win
    for r in range(1, 8):
        rot_scr[r] = pltpu.roll(win, n - r, 0)

    def block(rb, carry):
        r0 = pl.multiple_of(rb * CONV_ROWS, CONV_ROWS)
        acc = jnp.broadcast_to(bdw_ref[...], (CONV_ROWS, D_MODEL))
        for k in range(CONV_C):
            shift = k + 1
            acc = acc + wdw_ref[k:k + 1, :] * rot_scr[shift % 8, pl.ds(r0 + (shift // 8) * 8, CONV_ROWS), :]
        mu = jnp.mean(acc, axis=-1, keepdims=True)
        xc = acc - mu
        y = xc * lax.rsqrt(jnp.mean(xc * xc, axis=-1, keepdims=True) + EPS) * lng_ref[...] + lnb_ref[...]
        act_scr[pl.ds(r0, CONV_ROWS), :] = y * _sigmoid(y)
        return carry

    lax.fori_loop(0, tm // CONV_ROWS, block, 0)
    o_ref[...] = x_ref[...] + _dot(act_scr[...].astype(BF16), w2_ref[...]) + b2_ref[...]


def _conf_conv(u, x2d, seq, tm, wdw, bdw, lng, lnb, w2, b2):
    n_tok = x2d.shape[0]
    prev, main, nxt = _halo_specs(tm, D_MODEL, n_tok)
    vec = pl.BlockSpec((1, D_MODEL), lambda i: (0, 0))
    return pl.pallas_call(
        functools.partial(_conf_conv_kernel, tiles_per_seq=seq // tm),
        grid=(n_tok // tm,),
        in_specs=[prev, main, nxt, pl.BlockSpec((tm, D_MODEL), lambda i: (i, 0)),
                  pl.BlockSpec((CONV_C, D_MODEL), lambda i: (0, 0)), vec, vec, vec,
                  pl.BlockSpec((D_MODEL, D_MODEL), lambda i: (0, 0)), vec],
        out_specs=pl.BlockSpec((tm, D_MODEL), lambda i: (i, 0)),
        out_shape=jax.ShapeDtypeStruct((n_tok, D_MODEL), F32),
        scratch_shapes=[pltpu.VMEM((8, tm + 2 * HALO, D_MODEL), F32), pltpu.VMEM((tm, D_MODEL), F32)],
        compiler_params=_params(1),
        name="conf_conv",
    )(u, u, u, x2d, wdw, bdw, lng, lnb, w2, b2)


def _rope_tables(seq):
    pos = jnp.arange(seq)
    inv = ROPE_THETA ** (-jnp.arange(ROPE_PAIRS, dtype=F32) / ROPE_PAIRS)
    lane = np.arange(HEAD_DIM_A)
    section, half, pair = lane // (2 * ROPE_PAIRS), (lane // ROPE_PAIRS) % 2, lane % ROPE_PAIRS
    row_idx = (pos // GRID_W).astype(F32)[:, None]
    col_idx = (pos % GRID_W).astype(F32)[:, None]
    ang = jnp.where(jnp.asarray(section == 0)[None, :], row_idx, col_idx) * inv[pair][None, :]
    sign = jnp.asarray(np.where(half == 0, -1.0, 1.0), F32)[None, :]
    return jnp.tile(jnp.cos(ang), (1, 2)), jnp.tile(jnp.sin(ang) * sign, (1, 2))


def _segment_mean_matrix(n, width):
    seg = np.arange(n) // width
    return jnp.asarray((seg[:, None] == seg[None, :]).astype(np.float32) / width, BF16)


def _tile_size(seq, want):
    return min(want, seq)


def _trunk(x, p):
    batch, seq, _ = x.shape
    n_tok = batch * seq
    x2d = x.reshape(n_tok, D_MODEL)
    tm = _tile_size(seq, 512)
    cps = 2 if seq % (2 * CHUNK_B) == 0 else 1
    nc = seq // CHUNK_B

    q, kt, v, qb, kb, vb, ob, g = _in_proj(x2d, seq, tm, p["mix_norm_e"], p["w_in"], p["cos"], p["sin"],
                                           p["segq"], p["segk"], p["qg"], p["kg"], p["w_qk_conv"])
    out_a = _attention(q, kt, v, batch, seq, _tile_size(seq, 256))
    gcol = g.reshape(batch, nc, CHUNK_B, LANES)
    grow = jnp.swapaxes(gcol[..., :N_GATES_B], 2, 3)
    hf, hb = _mlstm(qb, kb, vb, gcol, grow, p["b_gates_row"], p["b_gates_col"], batch, seq, cps)
    x2d = _out_proj(x2d, out_a, hf, hb, ob, p["h_gain"], p["w_out"], tm)
    gated = _ffn_up(x2d, seq, tm, D_FF // 2, p["ffn_norm0"], p["w_up0"], p["w_dw_ff0"], p["b_dw_ff0"])
    x2d = _ffn_down(x2d, gated, p["w_down0"], tm)
    u = _conf_glu(x2d, p["mix_norm_o"], p["w_pw1"], p["b_pw1"], tm)
    x2d = _conf_conv(u, x2d, seq, _tile_size(seq, 256), p["w_dw_c"], p["b_dw_c"], p["ln_g"], p["ln_b"],
                     p["w_pw2"], p["b_pw2"])
    gated = _ffn_up(x2d, seq, tm, D_FF // 2, p["ffn_norm1"], p["w_up1"], p["w_dw_ff1"], p["b_dw_ff1"])
    x2d = _ffn_down(x2d, gated, p["w_down1"], tm)
    return x2d.reshape(batch, seq, D_MODEL)


def kernel(x_prompt, x_sample, mix_norm_e, w_in, q_gain_a, k_gain_a, w_qk_conv_b, b_gates_b, h_gain_b, w_out_e, mix_norm_o, w_pw1_c, b_pw1_c, w_dw_c, b_dw_c, ln_g_c, ln_b_c, w_pw2_c, b_pw2_c, ffn_norm, w_up, w_dw_ff, b_dw_ff, w_down):
    row = lambda a: a.reshape(1, -1).astype(F32)
    p = {
        "mix_norm_e": row(mix_norm_e[0]),
        "w_in": jnp.pad(w_in[0].astype(BF16), ((0, 0), (0, IN_COLS_PAD - IN_COLS))),
        "segq": _segment_mean_matrix(D_A, HEAD_DIM_A),
        "segk": _segment_mean_matrix(D_KV_A, HEAD_DIM_A),
        "qg": row(jnp.tile(q_gain_a[0], N_HEADS_A)),
        "kg": row(jnp.tile(k_gain_a[0], N_KV_HEADS_A)),
        "w_qk_conv": w_qk_conv_b[0].astype(F32),
        "b_gates_row": jnp.pad(row(b_gates_b[0]), ((0, 0), (0, LANES - N_GATES_B))),
        "b_gates_col": b_gates_b[0].reshape(-1, 1).astype(F32),
        "h_gain": row(h_gain_b[0]),
        "w_out": w_out_e[0].astype(BF16),
        "mix_norm_o": row(mix_norm_o[0]),
        "w_pw1": w_pw1_c[0].astype(BF16),
        "b_pw1": row(b_pw1_c[0]),
        "w_dw_c": w_dw_c[0].astype(F32),
        "b_dw_c": row(b_dw_c[0]),
        "ln_g": row(ln_g_c[0]),
        "ln_b": row(ln_b_c[0]),
        "w_pw2": w_pw2_c[0].astype(BF16),
        "b_pw2": row(b_pw2_c[0]),
    }
    for layer in range(2):
        p[f"ffn_norm{layer}"] = row(ffn_norm[layer])
        p[f"w_up{layer}"] = w_up[layer].astype(BF16)
        p[f"w_dw_ff{layer}"] = w_dw_ff[layer].astype(F32)
        p[f"b_dw_ff{layer}"] = row(b_dw_ff[layer])
        p[f"w_down{layer}"] = w_down[layer].astype(BF16)
    outs = []
    for x in (x_prompt, x_sample):
        p["cos"], p["sin"] = _rope_tables(x.shape[1])
        outs.append(_trunk(x, p))
    return tuple(outs)
```

```python
import functools

import numpy as np
import jax
import jax.numpy as jnp
from jax import lax
from jax.experimental import pallas as pl
from jax.experimental.pallas import tpu as pltpu

D_MODEL = 1024
GRID_W = 64
N_HEADS_A = 8
N_KV_HEADS_A = 2
HEAD_DIM_A = 64
D_A = N_HEADS_A * HEAD_DIM_A
D_KV_A = N_KV_HEADS_A * HEAD_DIM_A
ROPE_THETA = 10000.0
ROPE_PAIRS = HEAD_DIM_A // 4
N_HEADS_B = 4
HEAD_DIM_B = 128
D_B = N_HEADS_B * HEAD_DIM_B
CHUNK_B = 64
N_GATES_B = 4 * N_HEADS_B
IN_COLS = D_A + 2 * D_KV_A + 4 * D_B + N_GATES_B
IN_COLS_PAD = 2944
CONV_C = 31
D_FF = 2816
EPS = 1e-6
Q_SCALE = HEAD_DIM_A ** -0.5 * float(np.log2(np.e))

LANES = 128
HALO = 16
VMEM_LIMIT = 56 * 1024 * 1024

F32 = jnp.float32
BF16 = jnp.bfloat16

_O_QA, _O_KA, _O_VA, _O_QKB, _O_VB, _O_OB, _O_G = 0, 512, 640, 768, 1792, 2304, 2816


def _params(n_axes):
    return pltpu.CompilerParams(dimension_semantics=("arbitrary",) * n_axes,
                                vmem_limit_bytes=VMEM_LIMIT)


def _dot(a, b):
    return jnp.dot(a, b, preferred_element_type=F32)


def _rms(x, gain):
    ms = jnp.mean(x * x, axis=-1, keepdims=True)
    return x * lax.rsqrt(ms + EPS) * gain


def _sigmoid(x):
    return 1.0 / (1.0 + jnp.exp(-x))


def _normed_window(xp_ref, x_ref, xn_ref, gain, first, last):
    hp = jnp.where(first, 0.0, _rms(xp_ref[...], gain))
    hn = jnp.where(last, 0.0, _rms(xn_ref[...], gain))
    h = _rms(x_ref[...], gain)
    return jnp.concatenate([hp, h, hn], axis=0).astype(BF16)


def _conv3(u, cw, tm):
    n = u.shape[0]
    um = pltpu.roll(u, 1, 0)[HALO:HALO + tm]
    uc = u[HALO:HALO + tm]
    up = pltpu.roll(u, n - 1, 0)[HALO:HALO + tm]
    return um * cw[0:1] + uc * cw[1:2] + up * cw[2:3]


def _halo_specs(tm, d, n_tokens, axis=0, n_axes=1):
    r = tm // HALO
    last_blk = n_tokens // HALO - 1

    def pick(idx):
        return idx[axis]

    prev = pl.BlockSpec((HALO, d), lambda *idx: (jnp.maximum(pick(idx) * r - 1, 0), 0))
    main = pl.BlockSpec((tm, d), lambda *idx: (pick(idx), 0))
    nxt = pl.BlockSpec((HALO, d), lambda *idx: (jnp.minimum((pick(idx) + 1) * r, last_blk), 0))
    return prev, main, nxt


def _rope(xn, cos, sin, width):
    lane = lax.broadcasted_iota(jnp.int32, xn.shape, 1)
    first_half = (lane % (2 * ROPE_PAIRS)) < ROPE_PAIRS
    partner = jnp.where(first_half, pltpu.roll(xn, width - ROPE_PAIRS, 1), pltpu.roll(xn, ROPE_PAIRS, 1))
    return xn * cos + partner * sin


def _in_proj_kernel(xp_ref, x_ref, xn_ref, gain_ref, w_ref, cos_ref, sin_ref, segq_ref, segk_ref,
                    qg_ref, kg_ref, cw_ref, wvt_ref,
                    qt_out, k_out, vt_out, qb_out, kb_out, vb_out, ob_out, g_out, *, tiles_per_seq):
    i = pl.program_id(0)
    tm = x_ref.shape[0]
    first = (i % tiles_per_seq) == 0
    last = (i % tiles_per_seq) == tiles_per_seq - 1
    hext = _normed_window(xp_ref, x_ref, xn_ref, gain_ref[...], first, last)
    h = hext[HALO:HALO + tm]

    u = _dot(hext, w_ref[:, _O_QKB:_O_VB])
    c = _conv3(u, cw_ref[...], tm)
    act = c * _sigmoid(c)
    qb_out[...] = act[:, :D_B].astype(BF16)
    kb_out[...] = (act[:, D_B:] * (HEAD_DIM_B ** -0.5)).astype(BF16)
    vb_out[...] = _dot(h, w_ref[:, _O_VB:_O_OB]).astype(BF16)
    ob_out[...] = _dot(h, w_ref[:, _O_OB:_O_G])
    g_out[...] = _dot(h, w_ref[:, _O_G:IN_COLS_PAD])

    cos2 = cos_ref[...]
    sin2 = sin_ref[...]
    qa = _dot(h, w_ref[:, _O_QA:_O_KA])
    ms = _dot((qa * qa).astype(BF16), segq_ref[...])
    qn = qa * lax.rsqrt(ms + EPS) * qg_ref[...]
    cos = jnp.concatenate([cos2] * (D_A // LANES), axis=1)
    sin = jnp.concatenate([sin2] * (D_A // LANES), axis=1)
    qt_out[...] = (_rope(qn, cos, sin, D_A) * Q_SCALE).T.astype(BF16)

    ka = _dot(h, w_ref[:, _O_KA:_O_VA])
    msk = _dot((ka * ka).astype(BF16), segk_ref[...])
    kn = ka * lax.rsqrt(msk + EPS) * kg_ref[...]
    k_out[...] = _rope(kn, cos2, sin2, D_KV_A).astype(BF16)

    vt = lax.dot_general(wvt_ref[...], h, (((1,), (1,)), ((), ())), preferred_element_type=F32)
    vt_out[0] = vt.astype(BF16)


def _in_proj(x2d, seq, tm, gain, w_pad, cos, sin, segq, segk, qg, kg, cw, wvt):
    n_tok = x2d.shape[0]
    nt = n_tok // tm
    tps = seq // tm
    prev, main, nxt = _halo_specs(tm, D_MODEL, n_tok)
    const = lambda shape: pl.BlockSpec(shape, lambda i: (0,) * len(shape))
    rope_spec = pl.BlockSpec((tm, LANES), lambda i: (i % tps, 0))
    tok = lambda d: pl.BlockSpec((tm, d), lambda i: (i, 0))
    out_shape = (
        jax.ShapeDtypeStruct((D_A, n_tok), BF16),
        jax.ShapeDtypeStruct((n_tok, D_KV_A), BF16),
        jax.ShapeDtypeStruct((nt, D_KV_A, tm), BF16),
        jax.ShapeDtypeStruct((n_tok, D_B), BF16),
        jax.ShapeDtypeStruct((n_tok, D_B), BF16),
        jax.ShapeDtypeStruct((n_tok, D_B), BF16),
        jax.ShapeDtypeStruct((n_tok, D_B), F32),
        jax.ShapeDtypeStruct((n_tok, LANES), F32),
    )
    out_specs = (pl.BlockSpec((D_A, tm), lambda i: (0, i)), tok(D_KV_A),
                 pl.BlockSpec((1, D_KV_A, tm), lambda i: (i, 0, 0)),
                 tok(D_B), tok(D_B), tok(D_B), tok(D_B), tok(LANES))
    return pl.pallas_call(
        functools.partial(_in_proj_kernel, tiles_per_seq=tps),
        grid=(nt,),
        in_specs=[prev, main, nxt, const((1, D_MODEL)), const((D_MODEL, IN_COLS_PAD)), rope_spec, rope_spec,
                  const((D_A, D_A)), const((D_KV_A, D_KV_A)), const((1, D_A)), const((1, D_KV_A)),
                  const((3, 2 * D_B)), const((D_KV_A, D_MODEL))],
        out_specs=out_specs,
        out_shape=out_shape,
        compiler_params=_params(1),
        name="in_proj",
    )(x2d, x2d, x2d, gain, w_pad, cos, sin, segq, segk, qg, kg, cw, wvt)


def _attn_kernel(qt_ref, k_ref, vt_ref, o_ref, s_a, s_b, mx_a, mx_b, *, n_chunks, kc):
    tq = qt_ref.shape[1]
    dh = HEAD_DIM_A
    group = N_HEADS_A // N_KV_HEADS_A
    zeros = jnp.zeros((dh, tq), BF16)
    for g in range(N_KV_HEADS_A):
        rhs = []
        for h in range(g * group, (g + 1) * group):
            qh = qt_ref[h * dh:(h + 1) * dh, :]
            rhs.append(jnp.concatenate([qh, zeros] if g == 0 else [zeros, qh], axis=0))

        def scores(c, s_scr, mx_scr, rhs=rhs):
            kblk = k_ref[pl.ds(pl.multiple_of(c * kc, kc), kc), :]
            for j, r in enumerate(rhs):
                s = _dot(kblk, r)
                s_scr[j] = s
                mx_scr[j] = jnp.max(s, axis=0, keepdims=True)

        def update(c, s_scr, mx_scr, carry, g=g):
            vblk = vt_ref[c, g * dh:(g + 1) * dh, :]
            out = []
            for j, (m, l, acc) in enumerate(carry):
                m_new = jnp.maximum(m, mx_scr[j])
                p = jnp.exp2(s_scr[j] - m_new)
                alpha = jnp.exp2(m - m_new)
                l = alpha * l + jnp.sum(p, axis=0, keepdims=True)
                acc = alpha * acc + _dot(vblk, p.astype(BF16))
                out.append((m_new, l, acc))
            return tuple(out)

        def body(i, carry, scores=scores, update=update):
            c = 2 * i
            scores(c + 1, s_b, mx_b)
            carry = update(c, s_a, mx_a, carry)
            scores(c + 2, s_a, mx_a)
            return update(c + 1, s_b, mx_b, carry)

        init = (jnp.full((1, tq), -jnp.inf, F32), jnp.zeros((1, tq), F32), jnp.zeros((dh, tq), F32))
        scores(0, s_a, mx_a)
        carry = lax.fori_loop(0, n_chunks // 2 - 1, body, (init,) * group)
        scores(n_chunks - 1, s_b, mx_b)
        carry = update(n_chunks - 2, s_a, mx_a, carry)
        carry = update(n_chunks - 1, s_b, mx_b, carry)
        for pair in range(group // 2):
            ot = jnp.concatenate([acc / l for (_, l, acc) in carry[2 * pair:2 * pair + 2]], axis=0)
            col = (g * group // 2 + pair) * LANES
            o_ref[:, col:col + LANES] = ot.T.astype(BF16)


def _attention(qt, k, vt, batch, seq, tq):
    n_tok = k.shape[0]
    kc = vt.shape[2]
    nq = seq // tq
    n_chunks = seq // kc
    assert n_chunks % 2 == 0, "the key loop handles chunks in pairs"
    group = N_HEADS_A // N_KV_HEADS_A
    return pl.pallas_call(
        functools.partial(_attn_kernel, n_chunks=n_chunks, kc=kc),
        grid=(batch, nq),
        in_specs=[pl.BlockSpec((D_A, tq), lambda b, i: (0, b * nq + i)),
                  pl.BlockSpec((seq, D_KV_A), lambda b, i: (b, 0)),
                  pl.BlockSpec((n_chunks, D_KV_A, kc), lambda b, i: (b, 0, 0))],
        out_specs=pl.BlockSpec((tq, D_A), lambda b, i: (b * nq + i, 0)),
        out_shape=jax.ShapeDtypeStruct((n_tok, D_A), BF16),
        scratch_shapes=[pltpu.VMEM((group, kc, tq), F32), pltpu.VMEM((group, kc, tq), F32),
                        pltpu.VMEM((group, 1, tq), F32), pltpu.VMEM((group, 1, tq), F32)],
        compiler_params=_params(2),
        name="attention",
    )(qt, k, vt)


def _log_sigmoid(x):
    return jnp.minimum(x, 0.0) - jnp.log1p(jnp.exp(-jnp.abs(x)))


def _split3(x):
    hi = x.astype(BF16)
    r1 = x - hi.astype(F32)
    mid = r1.astype(BF16)
    lo = (r1 - mid.astype(F32)).astype(BF16)
    return hi, mid, lo


def _mlstm_kernel(qf_ref, kf_ref, vf_ref, gcf_ref, grf_ref, qb_ref, kb_ref, vb_ref, gcb_ref, grb_ref,
                  brow_ref, bcol_ref, hf_out, hb_out, ct_scr, m_scr, *, cps):
    j = pl.program_id(1)
    L = CHUNK_B
    dh = HEAD_DIM_B

    @pl.when(j == 0)
    def _():
        ct_scr[...] = jnp.zeros_like(ct_scr)
        m_scr[...] = jnp.zeros_like(m_scr)

    row = lax.broadcasted_iota(jnp.int32, (L, L), 0)
    col = lax.broadcasted_iota(jnp.int32, (L, L), 1)
    lower = col <= row
    upper = col >= row
    linc = jnp.where(lower, 1.0, 0.0).astype(BF16)
    uinc = jnp.where(upper, 1.0, 0.0).astype(BF16)
    ones_col = jnp.where(lax.broadcasted_iota(jnp.int32, (L, dh), 1) == 0, 1.0, 0.0).astype(BF16)

    dirs = ((qf_ref, kf_ref, vf_ref, gcf_ref, grf_ref, hf_out), (qb_ref, kb_ref, vb_ref, gcb_ref, grb_ref, hb_out))
    for cc in range(cps):
        for d, (q_ref, k_ref, v_ref, gc_ref, gr_ref, out_ref) in enumerate(dirs):
            ci = cc if d == 0 else cps - 1 - cc
            gcol = gc_ref[0, ci] + brow_ref[...]
            grow = gr_ref[0, ci] + bcol_ref[...]
            lf_c = _log_sigmoid(gcol)
            lf_r = _log_sigmoid(grow)
            tri_c = linc if d == 0 else uinc
            tri_r = uinc if d == 0 else linc
            b_c_all = sum(_dot(tri_c, piece) for piece in _split3(lf_c))
            b_r_all = sum(_dot(piece, tri_r) for piece in _split3(lf_r))
            mask = lower if d == 0 else upper
            rows = slice(ci * L, (ci + 1) * L)
            for hd in range(N_HEADS_B):
                unit = d * N_HEADS_B + hd
                ci_col = d * 2 * N_HEADS_B + hd
                cf_col = ci_col + N_HEADS_B
                lanes = slice(hd * dh, (hd + 1) * dh)
                i_c = gcol[:, ci_col:ci_col + 1]
                b_c = b_c_all[:, cf_col:cf_col + 1]
                i_r = grow[ci_col:ci_col + 1, :]
                b_r = b_r_all[cf_col:cf_col + 1, :]
                m = m_scr[unit][:, 0:1]
                q = q_ref[rows, lanes]
                k = k_ref[rows, lanes]
                v = v_ref[rows, lanes]

                log_d = jnp.where(mask, b_c - b_r + i_r, -jnp.inf)
                inter = b_c + m
                m_row = jnp.maximum(jnp.max(log_d, axis=-1, keepdims=True), inter)
                s = lax.dot_general(q, k, (((1,), (1,)), ((), ())), preferred_element_type=F32)
                s = s * jnp.exp(log_d - m_row)
                w_inter = jnp.exp(inter - m_row)
                ct = ct_scr[unit]
                v_aug = jnp.concatenate([v, ones_col], axis=1)
                tot = w_inter * _dot(q, ct.astype(BF16)) + _dot(s.astype(BF16), v_aug)
                num = tot[:, :dh]
                den = tot[:, dh:dh + 1]
                out_ref[rows, lanes] = num / jnp.maximum(jnp.abs(den), jnp.exp(-m_row))

                b_last = b_c[L - 1:L] if d == 0 else b_c[0:1]
                w_log = b_last - b_c + i_c
                m_new = jnp.maximum(b_last + m, jnp.max(w_log, axis=0, keepdims=True))
                w = jnp.exp(w_log - m_new)
                decay = jnp.exp(b_last + m - m_new)
                wv = (w * v_aug.astype(F32)).astype(BF16)
                upd = lax.dot_general(k, wv, (((0,), (0,)), ((), ())), preferred_element_type=F32)
                ct_scr[unit] = decay * ct + upd
                m_scr[unit] = jnp.broadcast_to(m_new, (1, LANES))


def _mlstm(qb, kb, vb, gcol, grow, brow, bcol, batch, seq, cps):
    n_tok = qb.shape[0]
    rows = cps * CHUNK_B
    ns = seq // rows
    fwd = lambda b, j: (b * ns + j, 0)
    bwd = lambda b, j: (b * ns + ns - 1 - j, 0)
    gfwd = lambda b, j: (b, j, 0, 0)
    gbwd = lambda b, j: (b, ns - 1 - j, 0, 0)
    tok = lambda im: pl.BlockSpec((rows, D_B), im)
    gc = lambda im: pl.BlockSpec((1, cps, CHUNK_B, LANES), im)
    gr = lambda im: pl.BlockSpec((1, cps, N_GATES_B, CHUNK_B), im)
    n_units = 2 * N_HEADS_B
    return pl.pallas_call(
        functools.partial(_mlstm_kernel, cps=cps),
        grid=(batch, ns),
        in_specs=[tok(fwd), tok(fwd), tok(fwd), gc(gfwd), gr(gfwd),
                  tok(bwd), tok(bwd), tok(bwd), gc(gbwd), gr(gbwd),
                  pl.BlockSpec((1, LANES), lambda b, j: (0, 0)),
                  pl.BlockSpec((N_GATES_B, 1), lambda b, j: (0, 0))],
        out_specs=(tok(fwd), tok(bwd)),
        out_shape=(jax.ShapeDtypeStruct((n_tok, D_B), F32), jax.ShapeDtypeStruct((n_tok, D_B), F32)),
        scratch_shapes=[pltpu.VMEM((n_units, HEAD_DIM_B, 2 * HEAD_DIM_B), F32),
                        pltpu.VMEM((n_units, 1, LANES), F32)],
        compiler_params=_params(2),
        name="mlstm",
    )(qb, kb, vb, gcol, grow, qb, kb, vb, gcol, grow, brow, bcol)


def _out_proj_kernel(x_ref, oa_ref, hf_ref, hb_ref, ob_ref, hg_ref, w_ref, o_ref):
    hsum = hf_ref[...] + hb_ref[...]
    parts = []
    for hd in range(N_HEADS_B):
        hh = hsum[:, hd * HEAD_DIM_B:(hd + 1) * HEAD_DIM_B]
        parts.append(hh * lax.rsqrt(jnp.mean(hh * hh, axis=-1, keepdims=True) + EPS))
    hn = jnp.concatenate(parts, axis=1) * hg_ref[...]
    out_b = (hn * _sigmoid(ob_ref[...])).astype(BF16)
    o_ref[...] = x_ref[...] + _dot(oa_ref[...], w_ref[:D_A]) + _dot(out_b, w_ref[D_A:])


def _out_proj(x2d, out_a, hf, hb, ob, hg, w, tm):
    n_tok = x2d.shape[0]
    tok = lambda d: pl.BlockSpec((tm, d), lambda i: (i, 0))
    return pl.pallas_call(
        _out_proj_kernel,
        grid=(n_tok // tm,),
        in_specs=[tok(D_MODEL), tok(D_A), tok(D_B), tok(D_B), tok(D_B),
                  pl.BlockSpec((1, D_B), lambda i: (0, 0)),
                  pl.BlockSpec((D_A + D_B, D_MODEL), lambda i: (0, 0))],
        out_specs=tok(D_MODEL),
        out_shape=jax.ShapeDtypeStruct((n_tok, D_MODEL), F32),
        compiler_params=_params(1),
        name="out_proj",
    )(x2d, out_a, hf, hb, ob, hg, w)


def _ffn_up_kernel(xp_ref, x_ref, xn_ref, gain_ref, wg_ref, wv_ref, cwg_ref, cwv_ref, cbg_ref, cbv_ref, o_ref,
                   *, tiles_per_seq):
    i = pl.program_id(1)
    tm = x_ref.shape[0]
    first = (i % tiles_per_seq) == 0
    last = (i % tiles_per_seq) == tiles_per_seq - 1
    hext = _normed_window(xp_ref, x_ref, xn_ref, gain_ref[...], first, last)
    gate = _conv3(_dot(hext, wg_ref[...]), cwg_ref[...], tm) + cbg_ref[...]
    val = _conv3(_dot(hext, wv_ref[...]), cwv_ref[...], tm) + cbv_ref[...]
    o_ref[...] = (gate * _sigmoid(gate) * val).astype(BF16)


def _ffn_up(x2d, seq, tm, tn, gain, w_up, cw, cb):
    n_tok = x2d.shape[0]
    nj = D_FF // tn
    prev, main, nxt = _halo_specs(tm, D_MODEL, n_tok, axis=1)
    col = lambda r, off: pl.BlockSpec((r, tn), lambda j, i: (0, off + j))
    return pl.pallas_call(
        functools.partial(_ffn_up_kernel, tiles_per_seq=seq // tm),
        grid=(nj, n_tok // tm),
        in_specs=[prev, main, nxt, pl.BlockSpec((1, D_MODEL), lambda j, i: (0, 0)),
                  col(D_MODEL, 0), col(D_MODEL, nj), col(3, 0), col(3, nj), col(1, 0), col(1, nj)],
        out_specs=pl.BlockSpec((tm, tn), lambda j, i: (i, j)),
        out_shape=jax.ShapeDtypeStruct((n_tok, D_FF), BF16),
        compiler_params=_params(2),
        name="ffn_up",
    )(x2d, x2d, x2d, gain, w_up, w_up, cw, cw, cb, cb)


def _ffn_down_kernel(x_ref, g_ref, w_ref, o_ref):
    o_ref[...] = x_ref[...] + _dot(g_ref[...], w_ref[...])


def _ffn_down(x2d, g, w, tm):
    n_tok = x2d.shape[0]
    return pl.pallas_call(
        _ffn_down_kernel,
        grid=(n_tok // tm,),
        in_specs=[pl.BlockSpec((tm, D_MODEL), lambda i: (i, 0)), pl.BlockSpec((tm, D_FF), lambda i: (i, 0)),
                  pl.BlockSpec((D_FF, D_MODEL), lambda i: (0, 0))],
        out_specs=pl.BlockSpec((tm, D_MODEL), lambda i: (i, 0)),
        out_shape=jax.ShapeDtypeStruct((n_tok, D_MODEL), F32),
        compiler_params=_params(1),
        name="ffn_down",
    )(x2d, g, w)


def _conf_glu_kernel(x_ref, gain_ref, w_ref, b_ref, o_ref):
    h = _rms(x_ref[...], gain_ref[...]).astype(BF16)
    u = _dot(h, w_ref[...]) + b_ref[...]
    o_ref[...] = u[:, :D_MODEL] * _sigmoid(u[:, D_MODEL:])


def _conf_glu(x2d, gain, w, b, tm):
    n_tok = x2d.shape[0]
    return pl.pallas_call(
        _conf_glu_kernel,
        grid=(n_tok // tm,),
        in_specs=[pl.BlockSpec((tm, D_MODEL), lambda i: (i, 0)), pl.BlockSpec((1, D_MODEL), lambda i: (0, 0)),
                  pl.BlockSpec((D_MODEL, 2 * D_MODEL), lambda i: (0, 0)),
                  pl.BlockSpec((1, 2 * D_MODEL), lambda i: (0, 0))],
        out_specs=pl.BlockSpec((tm, D_MODEL), lambda i: (i, 0)),
        out_shape=jax.ShapeDtypeStruct((n_tok, D_MODEL), F32),
        compiler_params=_params(1),
        name="conf_glu",
    )(x2d, gain, w, b)


SUBLANES = 8
CONV_ROWS = 8


def _conf_conv_kernel(up_ref, u_ref, un_ref, x_ref, wdw_ref, bdw_ref, lng_ref, lnb_ref, w2_ref, b2_ref, o_ref,
                      rot_scr, act_scr, *, tiles_per_seq):
    i = pl.program_id(0)
    tm = u_ref.shape[0]
    n = tm + 2 * HALO
    first = (i % tiles_per_seq) == 0
    last = (i % tiles_per_seq) == tiles_per_seq - 1
    win = jnp.concatenate([jnp.where(first, 0.0, up_ref[...]), u_ref[...], jnp.where(last, 0.0, un_ref[...])], axis=0)
    rot_scr[0] = win
    for r in range(1, 8):
        rot_scr[r] = pltpu.roll(win, n - r, 0)

    def block(rb, carry):
        r0 = pl.multiple_of(rb * CONV_ROWS, CONV_ROWS)
        accs = [None] * (CONV_ROWS // SUBLANES)
        for k in range(CONV_C):
            shift = k + 1
            wk = wdw_ref[k]
            for a in range(len(accs)):
                rows = pl.ds(r0 + (shift // SUBLANES + a) * SUBLANES, SUBLANES)
                term = wk * rot_scr[shift % SUBLANES, rows, :]
                accs[a] = term if accs[a] is None else accs[a] + term
        for a, acc in enumerate(accs):
            act_scr[pl.ds(r0 + a * SUBLANES, SUBLANES), :] = acc
        return carry

    lax.fori_loop(0, tm // CONV_ROWS, block, 0)
    conv = act_scr[...] + bdw_ref[...]
    xc = conv - jnp.mean(conv, axis=-1, keepdims=True)
    y = xc * lax.rsqrt(jnp.mean(xc * xc, axis=-1, keepdims=True) + EPS) * lng_ref[...] + lnb_ref[...]
    act = (y * _sigmoid(y)).astype(BF16)
    o_ref[...] = x_ref[...] + _dot(act, w2_ref[...]) + b2_ref[...]


def _conf_conv(u, x2d, seq, tm, wdw, bdw, lng, lnb, w2, b2):
    n_tok = x2d.shape[0]
    prev, main, nxt = _halo_specs(tm, D_MODEL, n_tok)
    vec = pl.BlockSpec((1, D_MODEL), lambda i: (0, 0))
    return pl.pallas_call(
        functools.partial(_conf_conv_kernel, tiles_per_seq=seq // tm),
        grid=(n_tok // tm,),
        in_specs=[prev, main, nxt, pl.BlockSpec((tm, D_MODEL), lambda i: (i, 0)),
                  pl.BlockSpec((CONV_C, SUBLANES, D_MODEL), lambda i: (0, 0, 0)), vec, vec, vec,
                  pl.BlockSpec((D_MODEL, D_MODEL), lambda i: (0, 0)), vec],
        out_specs=pl.BlockSpec((tm, D_MODEL), lambda i: (i, 0)),
        out_shape=jax.ShapeDtypeStruct((n_tok, D_MODEL), F32),
        scratch_shapes=[pltpu.VMEM((8, tm + 2 * HALO, D_MODEL), F32), pltpu.VMEM((tm, D_MODEL), F32)],
        compiler_params=_params(1),
        name="conf_conv",
    )(u, u, u, x2d, wdw, bdw, lng, lnb, w2, b2)


def _rope_tables(seq):
    pos = jnp.arange(seq)
    inv = ROPE_THETA ** (-jnp.arange(ROPE_PAIRS, dtype=F32) / ROPE_PAIRS)
    lane = np.arange(HEAD_DIM_A)
    section, half, pair = lane // (2 * ROPE_PAIRS), (lane // ROPE_PAIRS) % 2, lane % ROPE_PAIRS
    row_idx = (pos // GRID_W).astype(F32)[:, None]
    col_idx = (pos % GRID_W).astype(F32)[:, None]
    ang = jnp.where(jnp.asarray(section == 0)[None, :], row_idx, col_idx) * inv[pair][None, :]
    sign = jnp.asarray(np.where(half == 0, -1.0, 1.0), F32)[None, :]
    return jnp.tile(jnp.cos(ang), (1, 2)), jnp.tile(jnp.sin(ang) * sign, (1, 2))


def _segment_mean_matrix(n, width):
    seg = np.arange(n) // width
    return jnp.asarray((seg[:, None] == seg[None, :]).astype(np.float32) / width, BF16)


def _tile_size(seq, want):
    return min(want, seq)


def _trunk(x, p):
    batch, seq, _ = x.shape
    n_tok = batch * seq
    x2d = x.reshape(n_tok, D_MODEL)
    tm = _tile_size(seq, 512)
    cps = 2 if seq % (2 * CHUNK_B) == 0 else 1
    nc = seq // CHUNK_B

    qt, k, vt, qb, kb, vb, ob, g = _in_proj(x2d, seq, tm, p["mix_norm_e"], p["w_in"], p["cos"], p["sin"],
                                            p["segq"], p["segk"], p["qg"], p["kg"], p["w_qk_conv"], p["w_v_t"])
    out_a = _attention(qt, k, vt, batch, seq, _tile_size(seq, 256))
    gcol = g.reshape(batch, nc, CHUNK_B, LANES)
    grow = jnp.swapaxes(gcol[..., :N_GATES_B], 2, 3)
    hf, hb = _mlstm(qb, kb, vb, gcol, grow, p["b_gates_row"], p["b_gates_col"], batch, seq, cps)
    x2d = _out_proj(x2d, out_a, hf, hb, ob, p["h_gain"], p["w_out"], tm)
    gated = _ffn_up(x2d, seq, tm, D_FF // 2, p["ffn_norm0"], p["w_up0"], p["w_dw_ff0"], p["b_dw_ff0"])
    x2d = _ffn_down(x2d, gated, p["w_down0"], tm)
    u = _conf_glu(x2d, p["mix_norm_o"], p["w_pw1"], p["b_pw1"], tm)
    x2d = _conf_conv(u, x2d, seq, _tile_size(seq, 256), p["w_dw_c"], p["b_dw_c"], p["ln_g"], p["ln_b"],
                     p["w_pw2"], p["b_pw2"])
    gated = _ffn_up(x2d, seq, tm, D_FF // 2, p["ffn_norm1"], p["w_up1"], p["w_dw_ff1"], p["b_dw_ff1"])
    x2d = _ffn_down(x2d, gated, p["w_down1"], tm)
    return x2d.reshape(batch, seq, D_MODEL)


def kernel(x_prompt, x_sample, mix_norm_e, w_in, q_gain_a, k_gain_a, w_qk_conv_b, b_gates_b, h_gain_b, w_out_e, mix_norm_o, w_pw1_c, b_pw1_c, w_dw_c, b_dw_c, ln_g_c, ln_b_c, w_pw2_c, b_pw2_c, ffn_norm, w_up, w_dw_ff, b_dw_ff, w_down):
    row = lambda a: a.reshape(1, -1).astype(F32)
    p = {
        "mix_norm_e": row(mix_norm_e[0]),
        "w_in": jnp.pad(w_in[0].astype(BF16), ((0, 0), (0, IN_COLS_PAD - IN_COLS))),
        "segq": _segment_mean_matrix(D_A, HEAD_DIM_A),
        "segk": _segment_mean_matrix(D_KV_A, HEAD_DIM_A),
        "qg": row(jnp.tile(q_gain_a[0], N_HEADS_A)),
        "kg": row(jnp.tile(k_gain_a[0], N_KV_HEADS_A)),
        "w_qk_conv": w_qk_conv_b[0].astype(F32),
        "w_v_t": w_in[0][:, _O_VA:_O_QKB].T.astype(BF16),
        "b_gates_row": jnp.pad(row(b_gates_b[0]), ((0, 0), (0, LANES - N_GATES_B))),
        "b_gates_col": b_gates_b[0].reshape(-1, 1).astype(F32),
        "h_gain": row(h_gain_b[0]),
        "w_out": w_out_e[0].astype(BF16),
        "mix_norm_o": row(mix_norm_o[0]),
        "w_pw1": w_pw1_c[0].astype(BF16),
        "b_pw1": row(b_pw1_c[0]),
        "w_dw_c": jnp.broadcast_to(w_dw_c[0].astype(F32)[:, None, :], (CONV_C, SUBLANES, D_MODEL)),
        "b_dw_c": row(b_dw_c[0]),
        "ln_g": row(ln_g_c[0]),
        "ln_b": row(ln_b_c[0]),
        "w_pw2": w_pw2_c[0].astype(BF16),
        "b_pw2": row(b_pw2_c[0]),
    }
    for layer in range(2):
        p[f"ffn_norm{layer}"] = row(ffn_norm[layer])
        p[f"w_up{layer}"] = w_up[layer].astype(BF16)
        p[f"w_dw_ff{layer}"] = w_dw_ff[layer].astype(F32)
        p[f"b_dw_ff{layer}"] = row(b_dw_ff[layer])
        p[f"w_down{layer}"] = w_down[layer].astype(BF16)
    outs = []
    for x in (x_prompt, x_sample):
        p["cos"], p["sin"] = _rope_tables(x.shape[1])
        outs.append(_trunk(x, p))
    return tuple(outs)
```

```python
import functools

import numpy as np
import jax
import jax.numpy as jnp
from jax import lax
from jax.experimental import pallas as pl
from jax.experimental.pallas import tpu as pltpu

D_MODEL = 1024
GRID_W = 64
N_HEADS_A = 8
N_KV_HEADS_A = 2
HEAD_DIM_A = 64
D_A = N_HEADS_A * HEAD_DIM_A
D_KV_A = N_KV_HEADS_A * HEAD_DIM_A
ROPE_THETA = 10000.0
ROPE_PAIRS = HEAD_DIM_A // 4
N_HEADS_B = 4
HEAD_DIM_B = 128
D_B = N_HEADS_B * HEAD_DIM_B
CHUNK_B = 128
N_GATES_B = 4 * N_HEADS_B
IN_COLS = D_A + 2 * D_KV_A + 4 * D_B + N_GATES_B
IN_COLS_PAD = 2944
CONV_C = 31
D_FF = 2816
EPS = 1e-6
Q_SCALE = HEAD_DIM_A ** -0.5 * float(np.log2(np.e))

LANES = 128
HALO = 16
ONES_ROWS = 16
VMEM_LIMIT = 56 * 1024 * 1024

F32 = jnp.float32
BF16 = jnp.bfloat16

_O_QA, _O_KA, _O_VA, _O_QKB, _O_VB, _O_OB, _O_G = 0, 512, 640, 768, 1792, 2304, 2816


def _params(n_axes):
    return pltpu.CompilerParams(dimension_semantics=("arbitrary",) * n_axes,
                                vmem_limit_bytes=VMEM_LIMIT)


def _dot(a, b):
    return jnp.dot(a, b, preferred_element_type=F32)


def _rms(x, gain):
    ms = jnp.mean(x * x, axis=-1, keepdims=True)
    return x * lax.rsqrt(ms + EPS) * gain


def _sigmoid(x):
    return 1.0 / (1.0 + jnp.exp(-x))


def _normed_window(xp_ref, x_ref, xn_ref, gain, first, last):
    hp = jnp.where(first, 0.0, _rms(xp_ref[...], gain))
    hn = jnp.where(last, 0.0, _rms(xn_ref[...], gain))
    h = _rms(x_ref[...], gain)
    return jnp.concatenate([hp, h, hn], axis=0).astype(BF16)


def _conv3(u, cw, tm):
    n = u.shape[0]
    um = pltpu.roll(u, 1, 0)[HALO:HALO + tm]
    uc = u[HALO:HALO + tm]
    up = pltpu.roll(u, n - 1, 0)[HALO:HALO + tm]
    return um * cw[0:1] + uc * cw[1:2] + up * cw[2:3]


def _halo_specs(tm, d, n_tokens, axis=0, n_axes=1):
    r = tm // HALO
    last_blk = n_tokens // HALO - 1

    def pick(idx):
        return idx[axis]

    prev = pl.BlockSpec((HALO, d), lambda *idx: (jnp.maximum(pick(idx) * r - 1, 0), 0))
    main = pl.BlockSpec((tm, d), lambda *idx: (pick(idx), 0))
    nxt = pl.BlockSpec((HALO, d), lambda *idx: (jnp.minimum((pick(idx) + 1) * r, last_blk), 0))
    return prev, main, nxt


def _rope(xn, cos, sin, width):
    lane = lax.broadcasted_iota(jnp.int32, xn.shape, 1)
    first_half = (lane % (2 * ROPE_PAIRS)) < ROPE_PAIRS
    partner = jnp.where(first_half, pltpu.roll(xn, width - ROPE_PAIRS, 1), pltpu.roll(xn, ROPE_PAIRS, 1))
    return xn * cos + partner * sin


def _in_proj_kernel(xp_ref, x_ref, xn_ref, gain_ref, w_ref, cos_ref, sin_ref, segq_ref, segk_ref,
                    qg_ref, kg_ref, cw_ref, wvt_ref,
                    qt_out, k_out, vt_out, qb_out, kb_out, vb_out, ob_out, g_out, *, tiles_per_seq):
    i = pl.program_id(0)
    tm = x_ref.shape[0]
    first = (i % tiles_per_seq) == 0
    last = (i % tiles_per_seq) == tiles_per_seq - 1
    hext = _normed_window(xp_ref, x_ref, xn_ref, gain_ref[...], first, last)
    h = hext[HALO:HALO + tm]

    u = _dot(hext, w_ref[:, _O_QKB:_O_VB])
    c = _conv3(u, cw_ref[...], tm)
    act = c * _sigmoid(c)
    qb_out[...] = act[:, :D_B].astype(BF16)
    kb_out[...] = (act[:, D_B:] * (HEAD_DIM_B ** -0.5)).astype(BF16)
    vb_out[...] = _dot(h, w_ref[:, _O_VB:_O_OB]).astype(BF16)
    ob_out[...] = _dot(h, w_ref[:, _O_OB:_O_G])
    g_out[...] = _dot(h, w_ref[:, _O_G:IN_COLS_PAD])

    cos2 = cos_ref[...]
    sin2 = sin_ref[...]
    qa = _dot(h, w_ref[:, _O_QA:_O_KA])
    ms = _dot((qa * qa).astype(BF16), segq_ref[...])
    qn = qa * lax.rsqrt(ms + EPS) * qg_ref[...]
    cos = jnp.concatenate([cos2] * (D_A // LANES), axis=1)
    sin = jnp.concatenate([sin2] * (D_A // LANES), axis=1)
    qt_out[...] = (_rope(qn, cos, sin, D_A) * Q_SCALE).T.astype(BF16)

    ka = _dot(h, w_ref[:, _O_KA:_O_VA])
    msk = _dot((ka * ka).astype(BF16), segk_ref[...])
    kn = ka * lax.rsqrt(msk + EPS) * kg_ref[...]
    k_out[...] = _rope(kn, cos2, sin2, D_KV_A).astype(BF16)

    vt = lax.dot_general(wvt_ref[...], h, (((1,), (1,)), ((), ())), preferred_element_type=F32)
    vt_out[0] = vt.astype(BF16)


def _in_proj(x2d, seq, tm, gain, w_pad, cos, sin, segq, segk, qg, kg, cw, wvt):
    n_tok = x2d.shape[0]
    nt = n_tok // tm
    tps = seq // tm
    prev, main, nxt = _halo_specs(tm, D_MODEL, n_tok)
    const = lambda shape: pl.BlockSpec(shape, lambda i: (0,) * len(shape))
    rope_spec = pl.BlockSpec((tm, LANES), lambda i: (i % tps, 0))
    tok = lambda d: pl.BlockSpec((tm, d), lambda i: (i, 0))
    out_shape = (
        jax.ShapeDtypeStruct((D_A, n_tok), BF16),
        jax.ShapeDtypeStruct((n_tok, D_KV_A), BF16),
        jax.ShapeDtypeStruct((nt, D_KV_A, tm), BF16),
        jax.ShapeDtypeStruct((n_tok, D_B), BF16),
        jax.ShapeDtypeStruct((n_tok, D_B), BF16),
        jax.ShapeDtypeStruct((n_tok, D_B), BF16),
        jax.ShapeDtypeStruct((n_tok, D_B), F32),
        jax.ShapeDtypeStruct((n_tok, LANES), F32),
    )
    out_specs = (pl.BlockSpec((D_A, tm), lambda i: (0, i)), tok(D_KV_A),
                 pl.BlockSpec((1, D_KV_A, tm), lambda i: (i, 0, 0)),
                 tok(D_B), tok(D_B), tok(D_B), tok(D_B), tok(LANES))
    return pl.pallas_call(
        functools.partial(_in_proj_kernel, tiles_per_seq=tps),
        grid=(nt,),
        in_specs=[prev, main, nxt, const((1, D_MODEL)), const((D_MODEL, IN_COLS_PAD)), rope_spec, rope_spec,
                  const((D_A, D_A)), const((D_KV_A, D_KV_A)), const((1, D_A)), const((1, D_KV_A)),
                  const((3, 2 * D_B)), const((D_KV_A, D_MODEL))],
        out_specs=out_specs,
        out_shape=out_shape,
        compiler_params=_params(1),
        name="in_proj",
    )(x2d, x2d, x2d, gain, w_pad, cos, sin, segq, segk, qg, kg, cw, wvt)


def _attn_kernel(qt_ref, k_ref, vt_ref, o_ref, s_a, s_b, mx_a, mx_b, *, n_chunks, kc):
    tq = qt_ref.shape[1]
    dh = HEAD_DIM_A
    group = N_HEADS_A // N_KV_HEADS_A
    zeros = jnp.zeros((dh, tq), BF16)
    ones = jnp.ones((ONES_ROWS, kc), BF16)
    for g in range(N_KV_HEADS_A):
        rhs = []
        for h in range(g * group, (g + 1) * group):
            qh = qt_ref[h * dh:(h + 1) * dh, :]
            rhs.append(jnp.concatenate([qh, zeros] if g == 0 else [zeros, qh], axis=0))

        def score1(c, j, s_scr, mx_scr, rhs=rhs):
            kblk = k_ref[pl.ds(pl.multiple_of(c * kc, kc), kc), :]
            s = _dot(kblk, rhs[j])
            s_scr[j] = s
            mx_scr[j] = jnp.max(s, axis=0, keepdims=True)

        def update1(c, j, s_scr, mx_scr, state, g=g):
            vblk = jnp.concatenate([vt_ref[c, g * dh:(g + 1) * dh, :], ones], axis=0)
            m, acc = state
            m_new = jnp.maximum(m, mx_scr[j])
            p = jnp.exp2(s_scr[j] - m_new)
            acc = jnp.exp2(m - m_new) * acc + _dot(vblk, p.astype(BF16))
            return m_new, acc

        def scores(c, s_scr, mx_scr, score1=score1):
            for j in range(group):
                score1(c, j, s_scr, mx_scr)

        def update(c, s_scr, mx_scr, carry, update1=update1):
            return tuple(update1(c, j, s_scr, mx_scr, state) for j, state in enumerate(carry))

        def fused(c_next, s_next, mx_next, c, s_cur, mx_cur, carry, score1=score1, update1=update1):
            out = []
            for j, state in enumerate(carry):
                score1(c_next, j, s_next, mx_next)
                out.append(update1(c, j, s_cur, mx_cur, state))
            return tuple(out)

        def body(i, carry, fused=fused):
            c = 2 * i
            carry = fused(c + 1, s_b, mx_b, c, s_a, mx_a, carry)
            return fused(c + 2, s_a, mx_a, c + 1, s_b, mx_b, carry)

        init = (jnp.full((1, tq), -jnp.inf, F32), jnp.zeros((dh + ONES_ROWS, tq), F32))
        scores(0, s_a, mx_a)
        carry = lax.fori_loop(0, n_chunks // 2 - 1, body, (init,) * group)
        scores(n_chunks - 1, s_b, mx_b)
        carry = update(n_chunks - 2, s_a, mx_a, carry)
        carry = update(n_chunks - 1, s_b, mx_b, carry)
        for pair in range(group // 2):
            ot = jnp.concatenate([acc[:dh] / acc[dh:dh + 1] for (_, acc) in carry[2 * pair:2 * pair + 2]],
                                 axis=0)
            col = (g * group // 2 + pair) * LANES
            o_ref[:, col:col + LANES] = ot.T.astype(BF16)


def _attention(qt, k, vt, batch, seq, tq):
    n_tok = k.shape[0]
    kc = vt.shape[2]
    nq = seq // tq
    n_chunks = seq // kc
    assert n_chunks % 2 == 0, "the key loop handles chunks in pairs"
    group = N_HEADS_A // N_KV_HEADS_A
    return pl.pallas_call(
        functools.partial(_attn_kernel, n_chunks=n_chunks, kc=kc),
        grid=(batch, nq),
        in_specs=[pl.BlockSpec((D_A, tq), lambda b, i: (0, b * nq + i)),
                  pl.BlockSpec((seq, D_KV_A), lambda b, i: (b, 0)),
                  pl.BlockSpec((n_chunks, D_KV_A, kc), lambda b, i: (b, 0, 0))],
        out_specs=pl.BlockSpec((tq, D_A), lambda b, i: (b * nq + i, 0)),
        out_shape=jax.ShapeDtypeStruct((n_tok, D_A), BF16),
        scratch_shapes=[pltpu.VMEM((group, kc, tq), F32), pltpu.VMEM((group, kc, tq), F32),
                        pltpu.VMEM((group, 1, tq), F32), pltpu.VMEM((group, 1, tq), F32)],
        compiler_params=_params(2),
        name="attention",
    )(qt, k, vt)


def _log_sigmoid(x):
    return jnp.minimum(x, 0.0) - jnp.log1p(jnp.exp(-jnp.abs(x)))


def _split3(x):
    hi = x.astype(BF16)
    r1 = x - hi.astype(F32)
    mid = r1.astype(BF16)
    lo = (r1 - mid.astype(F32)).astype(BF16)
    return hi, mid, lo


def _mlstm_kernel(qf_ref, kf_ref, vf_ref, gcf_ref, grf_ref, qb_ref, kb_ref, vb_ref, gcb_ref, grb_ref,
                  brow_ref, bcol_ref, hf_out, hb_out, ct_scr, m_scr, *, cps):
    j = pl.program_id(1)
    L = CHUNK_B
    dh = HEAD_DIM_B

    @pl.when(j == 0)
    def _():
        ct_scr[...] = jnp.zeros_like(ct_scr)
        m_scr[...] = jnp.zeros_like(m_scr)

    row = lax.broadcasted_iota(jnp.int32, (L, L), 0)
    col = lax.broadcasted_iota(jnp.int32, (L, L), 1)
    lower = col <= row
    upper = col >= row
    linc = jnp.where(lower, 1.0, 0.0).astype(BF16)
    uinc = jnp.where(upper, 1.0, 0.0).astype(BF16)
    ones_col = jnp.where(lax.broadcasted_iota(jnp.int32, (L, dh), 1) == 0, 1.0, 0.0).astype(BF16)

    dirs = ((qf_ref, kf_ref, vf_ref, gcf_ref, grf_ref, hf_out), (qb_ref, kb_ref, vb_ref, gcb_ref, grb_ref, hb_out))
    for cc in range(cps):
        for d, (q_ref, k_ref, v_ref, gc_ref, gr_ref, out_ref) in enumerate(dirs):
            ci = cc if d == 0 else cps - 1 - cc
            gcol = gc_ref[0, ci] + brow_ref[...]
            grow = gr_ref[0, ci] + bcol_ref[...]
            lf_c = _log_sigmoid(gcol)
            lf_r = _log_sigmoid(grow)
            tri_c = linc if d == 0 else uinc
            tri_r = uinc if d == 0 else linc
            b_c_all = sum(_dot(tri_c, piece) for piece in _split3(lf_c))
            b_r_all = sum(_dot(piece, tri_r) for piece in _split3(lf_r))
            mask = lower if d == 0 else upper
            rows = slice(ci * L, (ci + 1) * L)
            for hd in range(N_HEADS_B):
                unit = d * N_HEADS_B + hd
                ci_col = d * 2 * N_HEADS_B + hd
                cf_col = ci_col + N_HEADS_B
                lanes = slice(hd * dh, (hd + 1) * dh)
                i_c = gcol[:, ci_col:ci_col + 1]
                b_c = b_c_all[:, cf_col:cf_col + 1]
                i_r = grow[ci_col:ci_col + 1, :]
                b_r = b_r_all[cf_col:cf_col + 1, :]
                q = q_ref[rows, lanes]
                k = k_ref[rows, lanes]
                v = v_ref[rows, lanes]
                v_aug = jnp.concatenate([v, ones_col], axis=1)

                log_d = jnp.where(mask, b_c - b_r + i_r, -jnp.inf)
                a = jnp.max(log_d, axis=-1, keepdims=True)
                s = lax.dot_general(q, k, (((1,), (1,)), ((), ())), preferred_element_type=F32)
                intra = _dot((s * jnp.exp(log_d - a)).astype(BF16), v_aug)
                b_last = b_c[L - 1:L] if d == 0 else b_c[0:1]
                w_log = b_last - b_c + i_c
                wmax = jnp.max(w_log, axis=0, keepdims=True)
                wv = (jnp.exp(w_log - wmax) * v_aug.astype(F32)).astype(BF16)
                upd = lax.dot_general(k, wv, (((0,), (0,)), ((), ())), preferred_element_type=F32)

                m = m_scr[unit][:, 0:1]
                ct = ct_scr[unit]
                inter = b_c + m
                m_row = jnp.maximum(a, inter)
                w_intra = jnp.exp(a - m_row)
                w_inter = jnp.exp(inter - m_row)
                cross = _dot(q, ct.astype(BF16))
                den = w_inter * cross[:, dh:dh + 1] + w_intra * intra[:, dh:dh + 1]
                inv = 1.0 / jnp.maximum(jnp.abs(den), jnp.exp(-m_row))
                out_ref[rows, lanes] = (w_inter * inv) * cross[:, :dh] + (w_intra * inv) * intra[:, :dh]

                m_new = jnp.maximum(b_last + m, wmax)
                ct_scr[unit] = jnp.exp(b_last + m - m_new) * ct + jnp.exp(wmax - m_new) * upd
                m_scr[unit] = jnp.broadcast_to(m_new, (1, LANES))


def _mlstm(qb, kb, vb, gcol, grow, brow, bcol, batch, seq, cps):
    n_tok = qb.shape[0]
    rows = cps * CHUNK_B
    ns = seq // rows
    fwd = lambda b, j: (b * ns + j, 0)
    bwd = lambda b, j: (b * ns + ns - 1 - j, 0)
    gfwd = lambda b, j: (b, j, 0, 0)
    gbwd = lambda b, j: (b, ns - 1 - j, 0, 0)
    tok = lambda im: pl.BlockSpec((rows, D_B), im)
    gc = lambda im: pl.BlockSpec((1, cps, CHUNK_B, LANES), im)
    gr = lambda im: pl.BlockSpec((1, cps, N_GATES_B, CHUNK_B), im)
    n_units = 2 * N_HEADS_B
    return pl.pallas_call(
        functools.partial(_mlstm_kernel, cps=cps),
        grid=(batch, ns),
        in_specs=[tok(fwd), tok(fwd), tok(fwd), gc(gfwd), gr(gfwd),
                  tok(bwd), tok(bwd), tok(bwd), gc(gbwd), gr(gbwd),
                  pl.BlockSpec((1, LANES), lambda b, j: (0, 0)),
                  pl.BlockSpec((N_GATES_B, 1), lambda b, j: (0, 0))],
        out_specs=(tok(fwd), tok(bwd)),
        out_shape=(jax.ShapeDtypeStruct((n_tok, D_B), F32), jax.ShapeDtypeStruct((n_tok, D_B), F32)),
        scratch_shapes=[pltpu.VMEM((n_units, HEAD_DIM_B, 2 * HEAD_DIM_B), F32),
                        pltpu.VMEM((n_units, 1, LANES), F32)],
        compiler_params=_params(2),
        name="mlstm",
    )(qb, kb, vb, gcol, grow, qb, kb, vb, gcol, grow, brow, bcol)


def _out_proj_kernel(x_ref, oa_ref, hf_ref, hb_ref, ob_ref, hg_ref, w_ref, o_ref):
    hsum = hf_ref[...] + hb_ref[...]
    parts = []
    for hd in range(N_HEADS_B):
        hh = hsum[:, hd * HEAD_DIM_B:(hd + 1) * HEAD_DIM_B]
        parts.append(hh * lax.rsqrt(jnp.mean(hh * hh, axis=-1, keepdims=True) + EPS))
    hn = jnp.concatenate(parts, axis=1) * hg_ref[...]
    out_b = (hn * _sigmoid(ob_ref[...])).astype(BF16)
    o_ref[...] = x_ref[...] + _dot(oa_ref[...], w_ref[:D_A]) + _dot(out_b, w_ref[D_A:])


def _out_proj(x2d, out_a, hf, hb, ob, hg, w, tm):
    n_tok = x2d.shape[0]
    tok = lambda d: pl.BlockSpec((tm, d), lambda i: (i, 0))
    return pl.pallas_call(
        _out_proj_kernel,
        grid=(n_tok // tm,),
        in_specs=[tok(D_MODEL), tok(D_A), tok(D_B), tok(D_B), tok(D_B),
                  pl.BlockSpec((1, D_B), lambda i: (0, 0)),
                  pl.BlockSpec((D_A + D_B, D_MODEL), lambda i: (0, 0))],
        out_specs=tok(D_MODEL),
        out_shape=jax.ShapeDtypeStruct((n_tok, D_MODEL), F32),
        compiler_params=_params(1),
        name="out_proj",
    )(x2d, out_a, hf, hb, ob, hg, w)


MXU_COLS = 256


def _ffn_up_kernel(xp_ref, x_ref, xn_ref, gain_ref, w_ref, cw_ref, cb_ref, o_ref, *, tiles_per_seq):
    i = pl.program_id(0)
    tm = x_ref.shape[0]
    first = (i % tiles_per_seq) == 0
    last = (i % tiles_per_seq) == tiles_per_seq - 1
    hext = _normed_window(xp_ref, x_ref, xn_ref, gain_ref[...], first, last)
    for lo, hi in ((0, 5 * MXU_COLS), (5 * MXU_COLS, D_FF)):
        gcols = slice(lo, hi)
        vcols = slice(D_FF + lo, D_FF + hi)
        gate = _conv3(_dot(hext, w_ref[:, gcols]), cw_ref[:, gcols], tm) + cb_ref[:, gcols]
        val = _conv3(_dot(hext, w_ref[:, vcols]), cw_ref[:, vcols], tm) + cb_ref[:, vcols]
        o_ref[:, gcols] = (gate * _sigmoid(gate) * val).astype(BF16)


def _ffn_up(x2d, seq, tm, gain, w_up, cw, cb):
    n_tok = x2d.shape[0]
    prev, main, nxt = _halo_specs(tm, D_MODEL, n_tok)
    full = lambda r: pl.BlockSpec((r, 2 * D_FF), lambda i: (0, 0))
    return pl.pallas_call(
        functools.partial(_ffn_up_kernel, tiles_per_seq=seq // tm),
        grid=(n_tok // tm,),
        in_specs=[prev, main, nxt, pl.BlockSpec((1, D_MODEL), lambda i: (0, 0)), full(D_MODEL), full(3), full(1)],
        out_specs=pl.BlockSpec((tm, D_FF), lambda i: (i, 0)),
        out_shape=jax.ShapeDtypeStruct((n_tok, D_FF), BF16),
        compiler_params=_params(1),
        name="ffn_up",
    )(x2d, x2d, x2d, gain, w_up, cw, cb)


def _ffn_down_kernel(x_ref, g_ref, w_ref, o_ref):
    o_ref[...] = x_ref[...] + _dot(g_ref[...], w_ref[...])


def _ffn_down(x2d, g, w, tm):
    n_tok = x2d.shape[0]
    return pl.pallas_call(
        _ffn_down_kernel,
        grid=(n_tok // tm,),
        in_specs=[pl.BlockSpec((tm, D_MODEL), lambda i: (i, 0)), pl.BlockSpec((tm, D_FF), lambda i: (i, 0)),
                  pl.BlockSpec((D_FF, D_MODEL), lambda i: (0, 0))],
        out_specs=pl.BlockSpec((tm, D_MODEL), lambda i: (i, 0)),
        out_shape=jax.ShapeDtypeStruct((n_tok, D_MODEL), F32),
        compiler_params=_params(1),
        name="ffn_down",
    )(x2d, g, w)


def _conf_glu_kernel(x_ref, gain_ref, w_ref, b_ref, o_ref):
    h = _rms(x_ref[...], gain_ref[...]).astype(BF16)
    u = _dot(h, w_ref[...]) + b_ref[...]
    o_ref[...] = u[:, :D_MODEL] * _sigmoid(u[:, D_MODEL:])


def _conf_glu(x2d, gain, w, b, tm):
    n_tok = x2d.shape[0]
    return pl.pallas_call(
        _conf_glu_kernel,
        grid=(n_tok // tm,),
        in_specs=[pl.BlockSpec((tm, D_MODEL), lambda i: (i, 0)), pl.BlockSpec((1, D_MODEL), lambda i: (0, 0)),
                  pl.BlockSpec((D_MODEL, 2 * D_MODEL), lambda i: (0, 0)),
                  pl.BlockSpec((1, 2 * D_MODEL), lambda i: (0, 0))],
        out_specs=pl.BlockSpec((tm, D_MODEL), lambda i: (i, 0)),
        out_shape=jax.ShapeDtypeStruct((n_tok, D_MODEL), F32),
        compiler_params=_params(1),
        name="conf_glu",
    )(x2d, gain, w, b)


SUBLANES = 8
CONV_ROWS = 8


def _conf_conv_kernel(up_ref, u_ref, un_ref, x_ref, wdw_ref, bdw_ref, lng_ref, lnb_ref, w2_ref, b2_ref, o_ref,
                      rot_scr, act_scr, *, tiles_per_seq):
    i = pl.program_id(0)
    tm = u_ref.shape[0]
    n = tm + 2 * HALO
    first = (i % tiles_per_seq) == 0
    last = (i % tiles_per_seq) == tiles_per_seq - 1
    win = jnp.concatenate([jnp.where(first, 0.0, up_ref[...]), u_ref[...], jnp.where(last, 0.0, un_ref[...])], axis=0)
    rot_scr[0] = win
    for r in range(1, 8):
        rot_scr[r] = pltpu.roll(win, n - r, 0)

    def block(rb, carry):
        r0 = pl.multiple_of(rb * CONV_ROWS, CONV_ROWS)
        accs = [None] * (CONV_ROWS // SUBLANES)
        for k in range(CONV_C):
            shift = k + 1
            wk = wdw_ref[k]
            for a in range(len(accs)):
                rows = pl.ds(r0 + (shift // SUBLANES + a) * SUBLANES, SUBLANES)
                term = wk * rot_scr[shift % SUBLANES, rows, :]
                accs[a] = term if accs[a] is None else accs[a] + term
        for a, acc in enumerate(accs):
            act_scr[pl.ds(r0 + a * SUBLANES, SUBLANES), :] = acc
        return carry

    lax.fori_loop(0, tm // CONV_ROWS, block, 0)
    conv = act_scr[...] + bdw_ref[...]
    xc = conv - jnp.mean(conv, axis=-1, keepdims=True)
    y = xc * lax.rsqrt(jnp.mean(xc * xc, axis=-1, keepdims=True) + EPS) * lng_ref[...] + lnb_ref[...]
    act = (y * _sigmoid(y)).astype(BF16)
    o_ref[...] = x_ref[...] + _dot(act, w2_ref[...]) + b2_ref[...]


def _conf_conv(u, x2d, seq, tm, wdw, bdw, lng, lnb, w2, b2):
    n_tok = x2d.shape[0]
    prev, main, nxt = _halo_specs(tm, D_MODEL, n_tok)
    vec = pl.BlockSpec((1, D_MODEL), lambda i: (0, 0))
    return pl.pallas_call(
        functools.partial(_conf_conv_kernel, tiles_per_seq=seq // tm),
        grid=(n_tok // tm,),
        in_specs=[prev, main, nxt, pl.BlockSpec((tm, D_MODEL), lambda i: (i, 0)),
                  pl.BlockSpec((CONV_C, SUBLANES, D_MODEL), lambda i: (0, 0, 0)), vec, vec, vec,
                  pl.BlockSpec((D_MODEL, D_MODEL), lambda i: (0, 0)), vec],
        out_specs=pl.BlockSpec((tm, D_MODEL), lambda i: (i, 0)),
        out_shape=jax.ShapeDtypeStruct((n_tok, D_MODEL), F32),
        scratch_shapes=[pltpu.VMEM((8, tm + 2 * HALO, D_MODEL), F32), pltpu.VMEM((tm, D_MODEL), F32)],
        compiler_params=_params(1),
        name="conf_conv",
    )(u, u, u, x2d, wdw, bdw, lng, lnb, w2, b2)


def _rope_tables(seq):
    pos = jnp.arange(seq)
    inv = ROPE_THETA ** (-jnp.arange(ROPE_PAIRS, dtype=F32) / ROPE_PAIRS)
    lane = np.arange(HEAD_DIM_A)
    section, half, pair = lane // (2 * ROPE_PAIRS), (lane // ROPE_PAIRS) % 2, lane % ROPE_PAIRS
    row_idx = (pos // GRID_W).astype(F32)[:, None]
    col_idx = (pos % GRID_W).astype(F32)[:, None]
    ang = jnp.where(jnp.asarray(section == 0)[None, :], row_idx, col_idx) * inv[pair][None, :]
    sign = jnp.asarray(np.where(half == 0, -1.0, 1.0), F32)[None, :]
    return jnp.tile(jnp.cos(ang), (1, 2)), jnp.tile(jnp.sin(ang) * sign, (1, 2))


def _segment_mean_matrix(n, width):
    seg = np.arange(n) // width
    return jnp.asarray((seg[:, None] == seg[None, :]).astype(np.float32) / width, BF16)


def _tile_size(seq, want):
    return min(want, seq)


def _trunk(x, p):
    batch, seq, _ = x.shape
    n_tok = batch * seq
    x2d = x.reshape(n_tok, D_MODEL)
    tm = _tile_size(seq, 512)
    cps = 2
    nc = seq // CHUNK_B

    qt, k, vt, qb, kb, vb, ob, g = _in_proj(x2d, seq, tm, p["mix_norm_e"], p["w_in"], p["cos"], p["sin"],
                                            p["segq"], p["segk"], p["qg"], p["kg"], p["w_qk_conv"], p["w_v_t"])
    out_a = _attention(qt, k, vt, batch, seq, _tile_size(seq, 256))
    gcol = g.reshape(batch, nc, CHUNK_B, LANES)
    grow = jnp.swapaxes(gcol[..., :N_GATES_B], 2, 3)
    hf, hb = _mlstm(qb, kb, vb, gcol, grow, p["b_gates_row"], p["b_gates_col"], batch, seq, cps)
    x2d = _out_proj(x2d, out_a, hf, hb, ob, p["h_gain"], p["w_out"], tm)
    gated = _ffn_up(x2d, seq, tm, p["ffn_norm0"], p["w_up0"], p["w_dw_ff0"], p["b_dw_ff0"])
    x2d = _ffn_down(x2d, gated, p["w_down0"], tm)
    u = _conf_glu(x2d, p["mix_norm_o"], p["w_pw1"], p["b_pw1"], tm)
    x2d = _conf_conv(u, x2d, seq, _tile_size(seq, 256), p["w_dw_c"], p["b_dw_c"], p["ln_g"], p["ln_b"],
                     p["w_pw2"], p["b_pw2"])
    gated = _ffn_up(x2d, seq, tm, p["ffn_norm1"], p["w_up1"], p["w_dw_ff1"], p["b_dw_ff1"])
    x2d = _ffn_down(x2d, gated, p["w_down1"], tm)
    return x2d.reshape(batch, seq, D_MODEL)


def kernel(x_prompt, x_sample, mix_norm_e, w_in, q_gain_a, k_gain_a, w_qk_conv_b, b_gates_b, h_gain_b, w_out_e, mix_norm_o, w_pw1_c, b_pw1_c, w_dw_c, b_dw_c, ln_g_c, ln_b_c, w_pw2_c, b_pw2_c, ffn_norm, w_up, w_dw_ff, b_dw_ff, w_down):
    row = lambda a: a.reshape(1, -1).astype(F32)
    p = {
        "mix_norm_e": row(mix_norm_e[0]),
        "w_in": jnp.pad(w_in[0].astype(BF16), ((0, 0), (0, IN_COLS_PAD - IN_COLS))),
        "segq": _segment_mean_matrix(D_A, HEAD_DIM_A),
        "segk": _segment_mean_matrix(D_KV_A, HEAD_DIM_A),
        "qg": row(jnp.tile(q_gain_a[0], N_HEADS_A)),
        "kg": row(jnp.tile(k_gain_a[0], N_KV_HEADS_A)),
        "w_qk_conv": w_qk_conv_b[0].astype(F32),
        "w_v_t": w_in[0][:, _O_VA:_O_QKB].T.astype(BF16),
        "b_gates_row": jnp.pad(row(b_gates_b[0]), ((0, 0), (0, LANES - N_GATES_B))),
        "b_gates_col": b_gates_b[0].reshape(-1, 1).astype(F32),
        "h_gain": row(h_gain_b[0]),
        "w_out": w_out_e[0].astype(BF16),
        "mix_norm_o": row(mix_norm_o[0]),
        "w_pw1": w_pw1_c[0].astype(BF16),
        "b_pw1": row(b_pw1_c[0]),
        "w_dw_c": jnp.broadcast_to(w_dw_c[0].astype(F32)[:, None, :], (CONV_C, SUBLANES, D_MODEL)),
        "b_dw_c": row(b_dw_c[0]),
        "ln_g": row(ln_g_c[0]),
        "ln_b": row(ln_b_c[0]),
        "w_pw2": w_pw2_c[0].astype(BF16),
        "b_pw2": row(b_pw2_c[0]),
    }
    for layer in range(2):
        p[f"ffn_norm{layer}"] = row(ffn_norm[layer])
        p[f"w_up{layer}"] = w_up[layer].astype(BF16)
        p[f"w_dw_ff{layer}"] = w_dw_ff[layer].astype(F32)
        p[f"b_dw_ff{layer}"] = row(b_dw_ff[layer])
        p[f"w_down{layer}"] = w_down[layer].astype(BF16)
    outs = []
    for x in (x_prompt, x_sample):
        p["cos"], p["sin"] = _rope_tables(x.shape[1])
        outs.append(_trunk(x, p))
    return tuple(outs)
```

```python
import functools

import numpy as np
import jax
import jax.numpy as jnp
from jax import lax
from jax.experimental import pallas as pl
from jax.experimental.pallas import tpu as pltpu

D_MODEL = 1024
GRID_W = 64
N_HEADS_A = 8
N_KV_HEADS_A = 2
HEAD_DIM_A = 64
D_A = N_HEADS_A * HEAD_DIM_A
D_KV_A = N_KV_HEADS_A * HEAD_DIM_A
ROPE_THETA = 10000.0
ROPE_PAIRS = HEAD_DIM_A // 4
N_HEADS_B = 4
HEAD_DIM_B = 128
D_B = N_HEADS_B * HEAD_DIM_B
CHUNK_B = 128
N_GATES_B = 4 * N_HEADS_B
IN_COLS = D_A + 2 * D_KV_A + 4 * D_B + N_GATES_B
IN_COLS_PAD = 2944
CONV_C = 31
D_FF = 2816
EPS = 1e-6
Q_SCALE = HEAD_DIM_A ** -0.5 * float(np.log2(np.e))

LANES = 128
HALO = 16
ONES_ROWS = 16
VMEM_LIMIT = 56 * 1024 * 1024

F32 = jnp.float32
BF16 = jnp.bfloat16

_O_QA, _O_KA, _O_VA, _O_QKB, _O_VB, _O_OB, _O_G = 0, 512, 640, 768, 1792, 2304, 2816


def _params(n_axes):
    return pltpu.CompilerParams(dimension_semantics=("arbitrary",) * n_axes,
                                vmem_limit_bytes=VMEM_LIMIT)


def _dot(a, b):
    return jnp.dot(a, b, preferred_element_type=F32)


def _rms(x, gain):
    ms = jnp.mean(x * x, axis=-1, keepdims=True)
    return x * lax.rsqrt(ms + EPS) * gain


def _sigmoid(x):
    return 1.0 / (1.0 + jnp.exp(-x))


def _normed_window(xp_ref, x_ref, xn_ref, gain, first, last):
    hp = jnp.where(first, 0.0, _rms(xp_ref[...], gain))
    hn = jnp.where(last, 0.0, _rms(xn_ref[...], gain))
    h = _rms(x_ref[...], gain)
    return jnp.concatenate([hp, h, hn], axis=0).astype(BF16)


def _conv3(u, cw, tm):
    n = u.shape[0]
    um = pltpu.roll(u, 1, 0)[HALO:HALO + tm]
    uc = u[HALO:HALO + tm]
    up = pltpu.roll(u, n - 1, 0)[HALO:HALO + tm]
    return um * cw[0:1] + uc * cw[1:2] + up * cw[2:3]


def _halo_specs(tm, d, n_tokens, axis=0, n_axes=1):
    r = tm // HALO
    last_blk = n_tokens // HALO - 1

    def pick(idx):
        return idx[axis]

    prev = pl.BlockSpec((HALO, d), lambda *idx: (jnp.maximum(pick(idx) * r - 1, 0), 0))
    main = pl.BlockSpec((tm, d), lambda *idx: (pick(idx), 0))
    nxt = pl.BlockSpec((HALO, d), lambda *idx: (jnp.minimum((pick(idx) + 1) * r, last_blk), 0))
    return prev, main, nxt


def _rope(xn, cos, sin, width):
    lane = lax.broadcasted_iota(jnp.int32, xn.shape, 1)
    first_half = (lane % (2 * ROPE_PAIRS)) < ROPE_PAIRS
    partner = jnp.where(first_half, pltpu.roll(xn, width - ROPE_PAIRS, 1), pltpu.roll(xn, ROPE_PAIRS, 1))
    return xn * cos + partner * sin


def _in_proj_kernel(xp_ref, x_ref, xn_ref, gain_ref, w_ref, cos_ref, sin_ref, segq_ref, segk_ref,
                    qg_ref, kg_ref, cw_ref, wvt_ref,
                    qt_out, k_out, vt_out, qb_out, kb_out, vb_out, ob_out, g_out, *, tiles_per_seq):
    i = pl.program_id(0)
    tm = x_ref.shape[0]
    first = (i % tiles_per_seq) == 0
    last = (i % tiles_per_seq) == tiles_per_seq - 1
    hext = _normed_window(xp_ref, x_ref, xn_ref, gain_ref[...], first, last)
    h = hext[HALO:HALO + tm]

    u = _dot(hext, w_ref[:, _O_QKB:_O_VB])
    c = _conv3(u, cw_ref[...], tm)
    act = c * _sigmoid(c)
    qb_out[...] = act[:, :D_B].astype(BF16)
    kb_out[...] = (act[:, D_B:] * (HEAD_DIM_B ** -0.5)).astype(BF16)
    vb_out[...] = _dot(h, w_ref[:, _O_VB:_O_OB]).astype(BF16)
    ob_out[...] = _dot(h, w_ref[:, _O_OB:_O_G])
    g_out[...] = _dot(h, w_ref[:, _O_G:IN_COLS_PAD])

    cos2 = cos_ref[...]
    sin2 = sin_ref[...]
    qa = _dot(h, w_ref[:, _O_QA:_O_KA])
    ms = _dot((qa * qa).astype(BF16), segq_ref[...])
    qn = qa * lax.rsqrt(ms + EPS) * qg_ref[...]
    cos = jnp.concatenate([cos2] * (D_A // LANES), axis=1)
    sin = jnp.concatenate([sin2] * (D_A // LANES), axis=1)
    qt_out[...] = (_rope(qn, cos, sin, D_A) * Q_SCALE).T.astype(BF16)

    ka = _dot(h, w_ref[:, _O_KA:_O_VA])
    msk = _dot((ka * ka).astype(BF16), segk_ref[...])
    kn = ka * lax.rsqrt(msk + EPS) * kg_ref[...]
    k_out[...] = _rope(kn, cos2, sin2, D_KV_A).astype(BF16)

    vt = lax.dot_general(wvt_ref[...], h, (((1,), (1,)), ((), ())), preferred_element_type=F32)
    vt_out[0] = vt.astype(BF16)


def _in_proj(x2d, seq, tm, gain, w_pad, cos, sin, segq, segk, qg, kg, cw, wvt):
    n_tok = x2d.shape[0]
    nt = n_tok // tm
    tps = seq // tm
    prev, main, nxt = _halo_specs(tm, D_MODEL, n_tok)
    const = lambda shape: pl.BlockSpec(shape, lambda i: (0,) * len(shape))
    rope_spec = pl.BlockSpec((tm, LANES), lambda i: (i % tps, 0))
    tok = lambda d: pl.BlockSpec((tm, d), lambda i: (i, 0))
    out_shape = (
        jax.ShapeDtypeStruct((D_A, n_tok), BF16),
        jax.ShapeDtypeStruct((n_tok, D_KV_A), BF16),
        jax.ShapeDtypeStruct((nt, D_KV_A, tm), BF16),
        jax.ShapeDtypeStruct((n_tok, D_B), BF16),
        jax.ShapeDtypeStruct((n_tok, D_B), BF16),
        jax.ShapeDtypeStruct((n_tok, D_B), BF16),
        jax.ShapeDtypeStruct((n_tok, D_B), F32),
        jax.ShapeDtypeStruct((n_tok, LANES), F32),
    )
    out_specs = (pl.BlockSpec((D_A, tm), lambda i: (0, i)), tok(D_KV_A),
                 pl.BlockSpec((1, D_KV_A, tm), lambda i: (i, 0, 0)),
                 tok(D_B), tok(D_B), tok(D_B), tok(D_B), tok(LANES))
    return pl.pallas_call(
        functools.partial(_in_proj_kernel, tiles_per_seq=tps),
        grid=(nt,),
        in_specs=[prev, main, nxt, const((1, D_MODEL)), const((D_MODEL, IN_COLS_PAD)), rope_spec, rope_spec,
                  const((D_A, D_A)), const((D_KV_A, D_KV_A)), const((1, D_A)), const((1, D_KV_A)),
                  const((3, 2 * D_B)), const((D_KV_A, D_MODEL))],
        out_specs=out_specs,
        out_shape=out_shape,
        compiler_params=_params(1),
        name="in_proj",
    )(x2d, x2d, x2d, gain, w_pad, cos, sin, segq, segk, qg, kg, cw, wvt)


def _attn_kernel(qt_ref, k_ref, vt_ref, o_ref, s_a, s_b, mx_a, mx_b, *, n_chunks, kc):
    tq = qt_ref.shape[1]
    dh = HEAD_DIM_A
    group = N_HEADS_A // N_KV_HEADS_A
    zeros = jnp.zeros((dh, tq), BF16)
    ones = jnp.ones((ONES_ROWS, kc), BF16)
    for g in range(N_KV_HEADS_A):
        rhs = []
        for h in range(g * group, (g + 1) * group):
            qh = qt_ref[h * dh:(h + 1) * dh, :]
            rhs.append(jnp.concatenate([qh, zeros] if g == 0 else [zeros, qh], axis=0))

        def score1(c, j, s_scr, mx_scr, rhs=rhs):
            kblk = k_ref[pl.ds(pl.multiple_of(c * kc, kc), kc), :]
            s = _dot(kblk, rhs[j])
            s_scr[j] = s
            mx_scr[j] = jnp.max(s, axis=0, keepdims=True)

        def update1(c, j, s_scr, mx_scr, state, g=g):
            vblk = jnp.concatenate([vt_ref[c, g * dh:(g + 1) * dh, :], ones], axis=0)
            m, acc = state
            m_new = jnp.maximum(m, mx_scr[j])
            p = jnp.exp2(s_scr[j] - m_new)
            acc = jnp.exp2(m - m_new) * acc + _dot(vblk, p.astype(BF16))
            return m_new, acc

        def scores(c, s_scr, mx_scr, score1=score1):
            for j in range(group):
                score1(c, j, s_scr, mx_scr)

        def update(c, s_scr, mx_scr, carry, update1=update1):
            return tuple(update1(c, j, s_scr, mx_scr, state) for j, state in enumerate(carry))

        def fused(c_next, s_next, mx_next, c, s_cur, mx_cur, carry, score1=score1, update1=update1):
            out = []
            for j, state in enumerate(carry):
                score1(c_next, j, s_next, mx_next)
                out.append(update1(c, j, s_cur, mx_cur, state))
            return tuple(out)

        def body(i, carry, fused=fused):
            c = 2 * i
            carry = fused(c + 1, s_b, mx_b, c, s_a, mx_a, carry)
            return fused(c + 2, s_a, mx_a, c + 1, s_b, mx_b, carry)

        init = (jnp.full((1, tq), -jnp.inf, F32), jnp.zeros((dh + ONES_ROWS, tq), F32))
        scores(0, s_a, mx_a)
        carry = lax.fori_loop(0, n_chunks // 2 - 1, body, (init,) * group)
        scores(n_chunks - 1, s_b, mx_b)
        carry = update(n_chunks - 2, s_a, mx_a, carry)
        carry = update(n_chunks - 1, s_b, mx_b, carry)
        for pair in range(group // 2):
            ot = jnp.concatenate([acc[:dh] / acc[dh:dh + 1] for (_, acc) in carry[2 * pair:2 * pair + 2]],
                                 axis=0)
            col = (g * group // 2 + pair) * LANES
            o_ref[:, col:col + LANES] = ot.T.astype(BF16)


def _attention(qt, k, vt, batch, seq, tq):
    n_tok = k.shape[0]
    kc = vt.shape[2]
    nq = seq // tq
    n_chunks = seq // kc
    assert n_chunks % 2 == 0, "the key loop handles chunks in pairs"
    group = N_HEADS_A // N_KV_HEADS_A
    return pl.pallas_call(
        functools.partial(_attn_kernel, n_chunks=n_chunks, kc=kc),
        grid=(batch, nq),
        in_specs=[pl.BlockSpec((D_A, tq), lambda b, i: (0, b * nq + i)),
                  pl.BlockSpec((seq, D_KV_A), lambda b, i: (b, 0)),
                  pl.BlockSpec((n_chunks, D_KV_A, kc), lambda b, i: (b, 0, 0))],
        out_specs=pl.BlockSpec((tq, D_A), lambda b, i: (b * nq + i, 0)),
        out_shape=jax.ShapeDtypeStruct((n_tok, D_A), BF16),
        scratch_shapes=[pltpu.VMEM((group, kc, tq), F32), pltpu.VMEM((group, kc, tq), F32),
                        pltpu.VMEM((group, 1, tq), F32), pltpu.VMEM((group, 1, tq), F32)],
        compiler_params=_params(2),
        name="attention",
    )(qt, k, vt)


def _log_sigmoid(x):
    return jnp.minimum(x, 0.0) - jnp.log1p(jnp.exp(-jnp.abs(x)))


def _split3(x):
    hi = x.astype(BF16)
    r1 = x - hi.astype(F32)
    mid = r1.astype(BF16)
    lo = (r1 - mid.astype(F32)).astype(BF16)
    return hi, mid, lo


def _mlstm_kernel(qf_ref, kf_ref, vf_ref, gcf_ref, grf_ref, qb_ref, kb_ref, vb_ref, gcb_ref, grb_ref,
                  brow_ref, bcol_ref, hf_out, hb_out, ct_scr, m_scr, *, cps):
    j = pl.program_id(1)
    L = CHUNK_B
    dh = HEAD_DIM_B

    @pl.when(j == 0)
    def _():
        ct_scr[...] = jnp.zeros_like(ct_scr)
        m_scr[...] = jnp.zeros_like(m_scr)

    row = lax.broadcasted_iota(jnp.int32, (L, L), 0)
    col = lax.broadcasted_iota(jnp.int32, (L, L), 1)
    lower = col <= row
    upper = col >= row
    linc = jnp.where(lower, 1.0, 0.0).astype(BF16)
    uinc = jnp.where(upper, 1.0, 0.0).astype(BF16)
    ones_col = jnp.where(lax.broadcasted_iota(jnp.int32, (L, dh), 1) == 0, 1.0, 0.0).astype(BF16)

    dirs = ((qf_ref, kf_ref, vf_ref, gcf_ref, grf_ref, hf_out), (qb_ref, kb_ref, vb_ref, gcb_ref, grb_ref, hb_out))
    for cc in range(cps):
        for d, (q_ref, k_ref, v_ref, gc_ref, gr_ref, out_ref) in enumerate(dirs):
            ci = cc if d == 0 else cps - 1 - cc
            gcol = gc_ref[0, ci] + brow_ref[...]
            grow = gr_ref[0, ci] + bcol_ref[...]
            lf_c = _log_sigmoid(gcol)
            lf_r = _log_sigmoid(grow)
            tri_c = linc if d == 0 else uinc
            tri_r = uinc if d == 0 else linc
            b_c_all = sum(_dot(tri_c, piece) for piece in _split3(lf_c))
            b_r_all = sum(_dot(piece, tri_r) for piece in _split3(lf_r))
            mask = lower if d == 0 else upper
            rows = slice(ci * L, (ci + 1) * L)
            for hd in range(N_HEADS_B):
                unit = d * N_HEADS_B + hd
                ci_col = d * 2 * N_HEADS_B + hd
                cf_col = ci_col + N_HEADS_B
                lanes = slice(hd * dh, (hd + 1) * dh)
                i_c = gcol[:, ci_col:ci_col + 1]
                b_c = b_c_all[:, cf_col:cf_col + 1]
                i_r = grow[ci_col:ci_col + 1, :]
                b_r = b_r_all[cf_col:cf_col + 1, :]
                q = q_ref[rows, lanes]
                k = k_ref[rows, lanes]
                v = v_ref[rows, lanes]
                v_aug = jnp.concatenate([v, ones_col], axis=1)

                log_d = jnp.where(mask, b_c - b_r + i_r, -jnp.inf)
                a = jnp.max(log_d, axis=-1, keepdims=True)
                s = lax.dot_general(q, k, (((1,), (1,)), ((), ())), preferred_element_type=F32)
                intra = _dot((s * jnp.exp(log_d - a)).astype(BF16), v_aug)
                b_last = b_c[L - 1:L] if d == 0 else b_c[0:1]
                w_log = b_last - b_c + i_c
                wmax = jnp.max(w_log, axis=0, keepdims=True)
                wv = (jnp.exp(w_log - wmax) * v_aug.astype(F32)).astype(BF16)
                upd = lax.dot_general(k, wv, (((0,), (0,)), ((), ())), preferred_element_type=F32)

                m = m_scr[unit][:, 0:1]
                ct = ct_scr[unit]
                inter = b_c + m
                m_row = jnp.maximum(a, inter)
                w_intra = jnp.exp(a - m_row)
                w_inter = jnp.exp(inter - m_row)
                cross = _dot(q, ct.astype(BF16))
                den = w_inter * cross[:, dh:dh + 1] + w_intra * intra[:, dh:dh + 1]
                inv = 1.0 / jnp.maximum(jnp.abs(den), jnp.exp(-m_row))
                out_ref[rows, lanes] = (w_inter * inv) * cross[:, :dh] + (w_intra * inv) * intra[:, :dh]

                m_new = jnp.maximum(b_last + m, wmax)
                ct_scr[unit] = jnp.exp(b_last + m - m_new) * ct + jnp.exp(wmax - m_new) * upd
                m_scr[unit] = jnp.broadcast_to(m_new, (1, LANES))


def _mlstm(qb, kb, vb, gcol, grow, brow, bcol, batch, seq, cps):
    n_tok = qb.shape[0]
    rows = cps * CHUNK_B
    ns = seq // rows
    fwd = lambda b, j: (b * ns + j, 0)
    bwd = lambda b, j: (b * ns + ns - 1 - j, 0)
    gfwd = lambda b, j: (b, j, 0, 0)
    gbwd = lambda b, j: (b, ns - 1 - j, 0, 0)
    tok = lambda im: pl.BlockSpec((rows, D_B), im)
    gc = lambda im: pl.BlockSpec((1, cps, CHUNK_B, LANES), im)
    gr = lambda im: pl.BlockSpec((1, cps, N_GATES_B, CHUNK_B), im)
    n_units = 2 * N_HEADS_B
    return pl.pallas_call(
        functools.partial(_mlstm_kernel, cps=cps),
        grid=(batch, ns),
        in_specs=[tok(fwd), tok(fwd), tok(fwd), gc(gfwd), gr(gfwd),
                  tok(bwd), tok(bwd), tok(bwd), gc(gbwd), gr(gbwd),
                  pl.BlockSpec((1, LANES), lambda b, j: (0, 0)),
                  pl.BlockSpec((N_GATES_B, 1), lambda b, j: (0, 0))],
        out_specs=(tok(fwd), tok(bwd)),
        out_shape=(jax.ShapeDtypeStruct((n_tok, D_B), F32), jax.ShapeDtypeStruct((n_tok, D_B), F32)),
        scratch_shapes=[pltpu.VMEM((n_units, HEAD_DIM_B, 2 * HEAD_DIM_B), F32),
                        pltpu.VMEM((n_units, 1, LANES), F32)],
        compiler_params=_params(2),
        name="mlstm",
    )(qb, kb, vb, gcol, grow, qb, kb, vb, gcol, grow, brow, bcol)


def _out_proj_kernel(x_ref, oa_ref, hf_ref, hb_ref, ob_ref, hg_ref, w_ref, o_ref):
    hsum = hf_ref[...] + hb_ref[...]
    parts = []
    for hd in range(N_HEADS_B):
        hh = hsum[:, hd * HEAD_DIM_B:(hd + 1) * HEAD_DIM_B]
        parts.append(hh * lax.rsqrt(jnp.mean(hh * hh, axis=-1, keepdims=True) + EPS))
    hn = jnp.concatenate(parts, axis=1) * hg_ref[...]
    out_b = (hn * _sigmoid(ob_ref[...])).astype(BF16)
    o_ref[...] = x_ref[...] + _dot(oa_ref[...], w_ref[:D_A]) + _dot(out_b, w_ref[D_A:])


def _out_proj(x2d, out_a, hf, hb, ob, hg, w, tm):
    n_tok = x2d.shape[0]
    tok = lambda d: pl.BlockSpec((tm, d), lambda i: (i, 0))
    return pl.pallas_call(
        _out_proj_kernel,
        grid=(n_tok // tm,),
        in_specs=[tok(D_MODEL), tok(D_A), tok(D_B), tok(D_B), tok(D_B),
                  pl.BlockSpec((1, D_B), lambda i: (0, 0)),
                  pl.BlockSpec((D_A + D_B, D_MODEL), lambda i: (0, 0))],
        out_specs=tok(D_MODEL),
        out_shape=jax.ShapeDtypeStruct((n_tok, D_MODEL), F32),
        compiler_params=_params(1),
        name="out_proj",
    )(x2d, out_a, hf, hb, ob, hg, w)


MXU_COLS = 256


def _ffn_kernel(xp_ref, x_ref, xn_ref, gain_ref, wu_ref, cw_ref, cb_ref, wd_ref, o_ref, *, tiles_per_seq):
    i = pl.program_id(0)
    tm = x_ref.shape[0]
    first = (i % tiles_per_seq) == 0
    last = (i % tiles_per_seq) == tiles_per_seq - 1
    hext = _normed_window(xp_ref, x_ref, xn_ref, gain_ref[...], first, last)
    acc = x_ref[...]
    for lo, hi in ((0, 5 * MXU_COLS), (5 * MXU_COLS, D_FF)):
        gcols = slice(lo, hi)
        vcols = slice(D_FF + lo, D_FF + hi)
        gate = _conv3(_dot(hext, wu_ref[:, gcols]), cw_ref[:, gcols], tm) + cb_ref[:, gcols]
        val = _conv3(_dot(hext, wu_ref[:, vcols]), cw_ref[:, vcols], tm) + cb_ref[:, vcols]
        acc = acc + _dot((gate * _sigmoid(gate) * val).astype(BF16), wd_ref[lo:hi, :])
    o_ref[...] = acc


def _ffn(x2d, seq, tm, gain, w_up, cw, cb, w_down):
    n_tok = x2d.shape[0]
    prev, main, nxt = _halo_specs(tm, D_MODEL, n_tok)
    full = lambda r: pl.BlockSpec((r, 2 * D_FF), lambda i: (0, 0))
    once = pl.Buffered(1)
    return pl.pallas_call(
        functools.partial(_ffn_kernel, tiles_per_seq=seq // tm),
        grid=(n_tok // tm,),
        in_specs=[prev, main, nxt, pl.BlockSpec((1, D_MODEL), lambda i: (0, 0)),
                  pl.BlockSpec((D_MODEL, 2 * D_FF), lambda i: (0, 0), pipeline_mode=once), full(3), full(1),
                  pl.BlockSpec((D_FF, D_MODEL), lambda i: (0, 0), pipeline_mode=once)],
        out_specs=pl.BlockSpec((tm, D_MODEL), lambda i: (i, 0)),
        out_shape=jax.ShapeDtypeStruct((n_tok, D_MODEL), F32),
        compiler_params=_params(1),
        name="ffn",
    )(x2d, x2d, x2d, gain, w_up, cw, cb, w_down)


def _conf_glu_kernel(x_ref, gain_ref, w_ref, b_ref, o_ref):
    h = _rms(x_ref[...], gain_ref[...]).astype(BF16)
    u = _dot(h, w_ref[...]) + b_ref[...]
    o_ref[...] = u[:, :D_MODEL] * _sigmoid(u[:, D_MODEL:])


def _conf_glu(x2d, gain, w, b, tm):
    n_tok = x2d.shape[0]
    return pl.pallas_call(
        _conf_glu_kernel,
        grid=(n_tok // tm,),
        in_specs=[pl.BlockSpec((tm, D_MODEL), lambda i: (i, 0)), pl.BlockSpec((1, D_MODEL), lambda i: (0, 0)),
                  pl.BlockSpec((D_MODEL, 2 * D_MODEL), lambda i: (0, 0)),
                  pl.BlockSpec((1, 2 * D_MODEL), lambda i: (0, 0))],
        out_specs=pl.BlockSpec((tm, D_MODEL), lambda i: (i, 0)),
        out_shape=jax.ShapeDtypeStruct((n_tok, D_MODEL), F32),
        compiler_params=_params(1),
        name="conf_glu",
    )(x2d, gain, w, b)


SUBLANES = 8
CONV_ROWS = 64


def _conf_conv_kernel(up_ref, u_ref, un_ref, x_ref, wdw_ref, bdw_ref, lng_ref, lnb_ref, w2_ref, b2_ref, o_ref,
                      rot_scr, act_scr, *, tiles_per_seq):
    i = pl.program_id(0)
    tm = u_ref.shape[0]
    n = tm + 2 * HALO
    first = (i % tiles_per_seq) == 0
    last = (i % tiles_per_seq) == tiles_per_seq - 1
    win = jnp.concatenate([jnp.where(first, 0.0, up_ref[...]), u_ref[...], jnp.where(last, 0.0, un_ref[...])], axis=0)
    rot_scr[0] = win
    for r in range(1, 8):
        rot_scr[r] = pltpu.roll(win, n - r, 0)

    for c in range(D_MODEL // LANES):
        lanes = slice(c * LANES, (c + 1) * LANES)
        taps = [wdw_ref[k, :, lanes] for k in range(CONV_C)]

        def block(rb, carry, lanes=lanes, taps=taps):
            r0 = pl.multiple_of(rb * CONV_ROWS, CONV_ROWS)
            for a in range(CONV_ROWS // SUBLANES):
                acc = None
                for k in range(CONV_C):
                    shift = k + 1
                    rows = pl.ds(r0 + (shift // SUBLANES + a) * SUBLANES, SUBLANES)
                    term = taps[k] * rot_scr[shift % SUBLANES, rows, lanes]
                    acc = term if acc is None else acc + term
                act_scr[pl.ds(r0 + a * SUBLANES, SUBLANES), lanes] = acc
            return carry

        lax.fori_loop(0, tm // CONV_ROWS, block, 0)
    conv = act_scr[...] + bdw_ref[...]
    xc = conv - jnp.mean(conv, axis=-1, keepdims=True)
    y = xc * lax.rsqrt(jnp.mean(xc * xc, axis=-1, keepdims=True) + EPS) * lng_ref[...] + lnb_ref[...]
    act = (y * _sigmoid(y)).astype(BF16)
    o_ref[...] = x_ref[...] + _dot(act, w2_ref[...]) + b2_ref[...]


def _conf_conv(u, x2d, seq, tm, wdw, bdw, lng, lnb, w2, b2):
    n_tok = x2d.shape[0]
    prev, main, nxt = _halo_specs(tm, D_MODEL, n_tok)
    vec = pl.BlockSpec((1, D_MODEL), lambda i: (0, 0))
    return pl.pallas_call(
        functools.partial(_conf_conv_kernel, tiles_per_seq=seq // tm),
        grid=(n_tok // tm,),
        in_specs=[prev, main, nxt, pl.BlockSpec((tm, D_MODEL), lambda i: (i, 0)),
                  pl.BlockSpec((CONV_C, SUBLANES, D_MODEL), lambda i: (0, 0, 0)), vec, vec, vec,
                  pl.BlockSpec((D_MODEL, D_MODEL), lambda i: (0, 0)), vec],
        out_specs=pl.BlockSpec((tm, D_MODEL), lambda i: (i, 0)),
        out_shape=jax.ShapeDtypeStruct((n_tok, D_MODEL), F32),
        scratch_shapes=[pltpu.VMEM((8, tm + 2 * HALO, D_MODEL), F32), pltpu.VMEM((tm, D_MODEL), F32)],
        compiler_params=_params(1),
        name="conf_conv",
    )(u, u, u, x2d, wdw, bdw, lng, lnb, w2, b2)


def _rope_tables(seq):
    pos = jnp.arange(seq)
    inv = ROPE_THETA ** (-jnp.arange(ROPE_PAIRS, dtype=F32) / ROPE_PAIRS)
    lane = np.arange(HEAD_DIM_A)
    section, half, pair = lane // (2 * ROPE_PAIRS), (lane // ROPE_PAIRS) % 2, lane % ROPE_PAIRS
    row_idx = (pos // GRID_W).astype(F32)[:, None]
    col_idx = (pos % GRID_W).astype(F32)[:, None]
    ang = jnp.where(jnp.asarray(section == 0)[None, :], row_idx, col_idx) * inv[pair][None, :]
    sign = jnp.asarray(np.where(half == 0, -1.0, 1.0), F32)[None, :]
    return jnp.tile(jnp.cos(ang), (1, 2)), jnp.tile(jnp.sin(ang) * sign, (1, 2))


def _segment_mean_matrix(n, width):
    seg = np.arange(n) // width
    return jnp.asarray((seg[:, None] == seg[None, :]).astype(np.float32) / width, BF16)


def _tile_size(seq, want):
    return min(want, seq)


def _trunk(x, p):
    batch, seq, _ = x.shape
    n_tok = batch * seq
    x2d = x.reshape(n_tok, D_MODEL)
    tm = _tile_size(seq, 512)
    cps = 2
    nc = seq // CHUNK_B

    qt, k, vt, qb, kb, vb, ob, g = _in_proj(x2d, seq, tm, p["mix_norm_e"], p["w_in"], p["cos"], p["sin"],
                                            p["segq"], p["segk"], p["qg"], p["kg"], p["w_qk_conv"], p["w_v_t"])
    out_a = _attention(qt, k, vt, batch, seq, _tile_size(seq, 512))
    gcol = g.reshape(batch, nc, CHUNK_B, LANES)
    grow = jnp.swapaxes(gcol[..., :N_GATES_B], 2, 3)
    hf, hb = _mlstm(qb, kb, vb, gcol, grow, p["b_gates_row"], p["b_gates_col"], batch, seq, cps)
    x2d = _out_proj(x2d, out_a, hf, hb, ob, p["h_gain"], p["w_out"], tm)
    x2d = _ffn(x2d, seq, tm, p["ffn_norm0"], p["w_up0"], p["w_dw_ff0"], p["b_dw_ff0"], p["w_down0"])
    u = _conf_glu(x2d, p["mix_norm_o"], p["w_pw1"], p["b_pw1"], tm)
    x2d = _conf_conv(u, x2d, seq, tm, p["w_dw_c"], p["b_dw_c"], p["ln_g"], p["ln_b"],
                     p["w_pw2"], p["b_pw2"])
    x2d = _ffn(x2d, seq, tm, p["ffn_norm1"], p["w_up1"], p["w_dw_ff1"], p["b_dw_ff1"], p["w_down1"])
    return x2d.reshape(batch, seq, D_MODEL)


def kernel(x_prompt, x_sample, mix_norm_e, w_in, q_gain_a, k_gain_a, w_qk_conv_b, b_gates_b, h_gain_b, w_out_e, mix_norm_o, w_pw1_c, b_pw1_c, w_dw_c, b_dw_c, ln_g_c, ln_b_c, w_pw2_c, b_pw2_c, ffn_norm, w_up, w_dw_ff, b_dw_ff, w_down):
    row = lambda a: a.reshape(1, -1).astype(F32)
    p = {
        "mix_norm_e": row(mix_norm_e[0]),
        "w_in": jnp.pad(w_in[0].astype(BF16), ((0, 0), (0, IN_COLS_PAD - IN_COLS))),
        "segq": _segment_mean_matrix(D_A, HEAD_DIM_A),
        "segk": _segment_mean_matrix(D_KV_A, HEAD_DIM_A),
        "qg": row(jnp.tile(q_gain_a[0], N_HEADS_A)),
        "kg": row(jnp.tile(k_gain_a[0], N_KV_HEADS_A)),
        "w_qk_conv": w_qk_conv_b[0].astype(F32),
        "w_v_t": w_in[0][:, _O_VA:_O_QKB].T.astype(BF16),
        "b_gates_row": jnp.pad(row(b_gates_b[0]), ((0, 0), (0, LANES - N_GATES_B))),
        "b_gates_col": b_gates_b[0].reshape(-1, 1).astype(F32),
        "h_gain": row(h_gain_b[0]),
        "w_out": w_out_e[0].astype(BF16),
        "mix_norm_o": row(mix_norm_o[0]),
        "w_pw1": w_pw1_c[0].astype(BF16),
        "b_pw1": row(b_pw1_c[0]),
        "w_dw_c": jnp.broadcast_to(w_dw_c[0].astype(F32)[:, None, :], (CONV_C, SUBLANES, D_MODEL)),
        "b_dw_c": row(b_dw_c[0]),
        "ln_g": row(ln_g_c[0]),
        "ln_b": row(ln_b_c[0]),
        "w_pw2": w_pw2_c[0].astype(BF16),
        "b_pw2": row(b_pw2_c[0]),
    }
    for layer in range(2):
        p[f"ffn_norm{layer}"] = row(ffn_norm[layer])
        p[f"w_up{layer}"] = w_up[layer].astype(BF16)
        p[f"w_dw_ff{layer}"] = w_dw_ff[layer].astype(F32)
        p[f"b_dw_ff{layer}"] = row(b_dw_ff[layer])
        p[f"w_down{layer}"] = w_down[layer].astype(BF16)
    outs = []
    for x in (x_prompt, x_sample):
        p["cos"], p["sin"] = _rope_tables(x.shape[1])
        outs.append(_trunk(x, p))
    return tuple(outs)
```

```python
import functools

import numpy as np
import jax
import jax.numpy as jnp
from jax import lax
from jax.experimental import pallas as pl
from jax.experimental.pallas import tpu as pltpu

D_MODEL = 1024
GRID_W = 64
N_HEADS_A = 8
N_KV_HEADS_A = 2
HEAD_DIM_A = 64
D_A = N_HEADS_A * HEAD_DIM_A
D_KV_A = N_KV_HEADS_A * HEAD_DIM_A
ROPE_THETA = 10000.0
ROPE_PAIRS = HEAD_DIM_A // 4
N_HEADS_B = 4
HEAD_DIM_B = 128
D_B = N_HEADS_B * HEAD_DIM_B
CHUNK_B = 128
N_GATES_B = 4 * N_HEADS_B
IN_COLS = D_A + 2 * D_KV_A + 4 * D_B + N_GATES_B
IN_COLS_PAD = 2944
CONV_C = 31
D_FF = 2816
EPS = 1e-6
Q_SCALE = HEAD_DIM_A ** -0.5 * float(np.log2(np.e))

LANES = 128
SUBLANES = 8
HALO = 16
ONES_ROWS = 16
VMEM_LIMIT = 56 * 1024 * 1024

F32 = jnp.float32
BF16 = jnp.bfloat16

_O_QA, _O_KA, _O_VA, _O_QKB, _O_VB, _O_OB, _O_G = 0, 512, 640, 768, 1792, 2304, 2816


def _params(n_axes):
    return pltpu.CompilerParams(dimension_semantics=("arbitrary",) * n_axes,
                                vmem_limit_bytes=VMEM_LIMIT)


def _dot(a, b):
    return jnp.dot(a, b, preferred_element_type=F32)


def _rms(x, gain):
    ms = jnp.mean(x * x, axis=-1, keepdims=True)
    return x * lax.rsqrt(ms + EPS) * gain


def _sigmoid(x):
    return 1.0 / (1.0 + jnp.exp(-x))


def _normed_window(xp_ref, x_ref, xn_ref, gain, first, last):
    hp = jnp.where(first, 0.0, _rms(xp_ref[...], gain))
    hn = jnp.where(last, 0.0, _rms(xn_ref[...], gain))
    h = _rms(x_ref[...], gain)
    return jnp.concatenate([hp, h, hn], axis=0).astype(BF16)


def _conv3(u, cw, tm):
    n = u.shape[0]
    um = pltpu.roll(u, 1, 0)[HALO:HALO + tm]
    uc = u[HALO:HALO + tm]
    up = pltpu.roll(u, n - 1, 0)[HALO:HALO + tm]
    return um * cw[0:1] + uc * cw[1:2] + up * cw[2:3]


def _halo_specs(tm, d, n_tokens, axis=0, n_axes=1):
    r = tm // HALO
    last_blk = n_tokens // HALO - 1

    def pick(idx):
        return idx[axis]

    prev = pl.BlockSpec((HALO, d), lambda *idx: (jnp.maximum(pick(idx) * r - 1, 0), 0))
    main = pl.BlockSpec((tm, d), lambda *idx: (pick(idx), 0))
    nxt = pl.BlockSpec((HALO, d), lambda *idx: (jnp.minimum((pick(idx) + 1) * r, last_blk), 0))
    return prev, main, nxt


def _rope(xn, cos, sin, width):
    lane = lax.broadcasted_iota(jnp.int32, xn.shape, 1)
    first_half = (lane % (2 * ROPE_PAIRS)) < ROPE_PAIRS
    partner = jnp.where(first_half, pltpu.roll(xn, width - ROPE_PAIRS, 1), pltpu.roll(xn, ROPE_PAIRS, 1))
    return xn * cos + partner * sin


def _in_proj_kernel(xp_ref, x_ref, xn_ref, gain_ref, w_ref, cos_ref, sin_ref, segq_ref, segk_ref,
                    qg_ref, kg_ref, cw_ref, wvt_ref,
                    qt_out, k_out, vt_out, qb_out, kbt_out, vb_out, ob_out, g_out, *, tiles_per_seq):
    i = pl.program_id(0)
    tm = x_ref.shape[0]
    first = (i % tiles_per_seq) == 0
    last = (i % tiles_per_seq) == tiles_per_seq - 1
    hext = _normed_window(xp_ref, x_ref, xn_ref, gain_ref[...], first, last)
    h = hext[HALO:HALO + tm]

    u = _dot(hext, w_ref[:, _O_QKB:_O_VB])
    c = _conv3(u, cw_ref[...], tm)
    act = c * _sigmoid(c)
    qb_out[...] = act[:, :D_B].astype(BF16)
    kbt_out[...] = (act[:, D_B:] * (HEAD_DIM_B ** -0.5)).T.astype(BF16)
    vb_out[...] = _dot(h, w_ref[:, _O_VB:_O_OB]).astype(BF16)
    ob_out[...] = _dot(h, w_ref[:, _O_OB:_O_G])
    g_out[...] = _dot(h, w_ref[:, _O_G:IN_COLS_PAD])

    cos2 = cos_ref[...]
    sin2 = sin_ref[...]
    qa = _dot(h, w_ref[:, _O_QA:_O_KA])
    ms = _dot((qa * qa).astype(BF16), segq_ref[...])
    qn = qa * lax.rsqrt(ms + EPS) * qg_ref[...]
    cos = jnp.concatenate([cos2] * (D_A // LANES), axis=1)
    sin = jnp.concatenate([sin2] * (D_A // LANES), axis=1)
    qt_out[...] = (_rope(qn, cos, sin, D_A) * Q_SCALE).T.astype(BF16)

    ka = _dot(h, w_ref[:, _O_KA:_O_VA])
    msk = _dot((ka * ka).astype(BF16), segk_ref[...])
    kn = ka * lax.rsqrt(msk + EPS) * kg_ref[...]
    k_out[...] = _rope(kn, cos2, sin2, D_KV_A).astype(BF16)

    vt = lax.dot_general(wvt_ref[...], h, (((1,), (1,)), ((), ())), preferred_element_type=F32)
    vt_out[0] = vt.astype(BF16)


def _in_proj(x2d, seq, tm, gain, w_pad, cos, sin, segq, segk, qg, kg, cw, wvt):
    n_tok = x2d.shape[0]
    nt = n_tok // tm
    tps = seq // tm
    prev, main, nxt = _halo_specs(tm, D_MODEL, n_tok)
    const = lambda shape: pl.BlockSpec(shape, lambda i: (0,) * len(shape))
    rope_spec = pl.BlockSpec((tm, LANES), lambda i: (i % tps, 0))
    tok = lambda d: pl.BlockSpec((tm, d), lambda i: (i, 0))
    out_shape = (
        jax.ShapeDtypeStruct((D_A, n_tok), BF16),
        jax.ShapeDtypeStruct((n_tok, D_KV_A), BF16),
        jax.ShapeDtypeStruct((nt, D_KV_A, tm), BF16),
        jax.ShapeDtypeStruct((n_tok, D_B), BF16),
        jax.ShapeDtypeStruct((D_B, n_tok), BF16),
        jax.ShapeDtypeStruct((n_tok, D_B), BF16),
        jax.ShapeDtypeStruct((n_tok, D_B), F32),
        jax.ShapeDtypeStruct((n_tok, LANES), F32),
    )
    out_specs = (pl.BlockSpec((D_A, tm), lambda i: (0, i)), tok(D_KV_A),
                 pl.BlockSpec((1, D_KV_A, tm), lambda i: (i, 0, 0)),
                 tok(D_B), pl.BlockSpec((D_B, tm), lambda i: (0, i)), tok(D_B), tok(D_B), tok(LANES))
    return pl.pallas_call(
        functools.partial(_in_proj_kernel, tiles_per_seq=tps),
        grid=(nt,),
        in_specs=[prev, main, nxt, const((1, D_MODEL)), const((D_MODEL, IN_COLS_PAD)), rope_spec, rope_spec,
                  const((D_A, D_A)), const((D_KV_A, D_KV_A)), const((1, D_A)), const((1, D_KV_A)),
                  const((3, 2 * D_B)), const((D_KV_A, D_MODEL))],
        out_specs=out_specs,
        out_shape=out_shape,
        compiler_params=_params(1),
        name="in_proj",
    )(x2d, x2d, x2d, gain, w_pad, cos, sin, segq, segk, qg, kg, cw, wvt)


def _attn_kernel(qt_ref, k_ref, vt_ref, o_ref, s_a, s_b, mx_a, mx_b, *, n_chunks, kc):
    tq = qt_ref.shape[1]
    dh = HEAD_DIM_A
    group = N_HEADS_A // N_KV_HEADS_A
    zeros = jnp.zeros((dh, tq), BF16)
    ones = jnp.ones((ONES_ROWS, kc), BF16)
    for g in range(N_KV_HEADS_A):
        rhs = []
        for h in range(g * group, (g + 1) * group):
            qh = qt_ref[h * dh:(h + 1) * dh, :]
            rhs.append(jnp.concatenate([qh, zeros] if g == 0 else [zeros, qh], axis=0))

        def score1(c, j, s_scr, mx_scr, rhs=rhs):
            kblk = k_ref[pl.ds(pl.multiple_of(c * kc, kc), kc), :]
            s = _dot(kblk, rhs[j])
            s_scr[j] = s
            mx_scr[j] = jnp.max(s, axis=0, keepdims=True)

        def update1(c, j, s_scr, mx_scr, state, g=g):
            vblk = jnp.concatenate([vt_ref[c, g * dh:(g + 1) * dh, :], ones], axis=0)
            m, acc = state
            m_new = jnp.maximum(m, mx_scr[j])
            p = jnp.exp2(s_scr[j] - m_new)
            acc = jnp.exp2(m - m_new) * acc + _dot(vblk, p.astype(BF16))
            return m_new, acc

        def scores(c, s_scr, mx_scr, score1=score1):
            for j in range(group):
                score1(c, j, s_scr, mx_scr)

        def update(c, s_scr, mx_scr, carry, update1=update1):
            return tuple(update1(c, j, s_scr, mx_scr, state) for j, state in enumerate(carry))

        def fused(c_next, s_next, mx_next, c, s_cur, mx_cur, carry, score1=score1, update1=update1):
            out = []
            for j, state in enumerate(carry):
                score1(c_next, j, s_next, mx_next)
                out.append(update1(c, j, s_cur, mx_cur, state))
            return tuple(out)

        def body(i, carry, fused=fused):
            c = 2 * i
            carry = fused(c + 1, s_b, mx_b, c, s_a, mx_a, carry)
            return fused(c + 2, s_a, mx_a, c + 1, s_b, mx_b, carry)

        init = (jnp.full((1, tq), -jnp.inf, F32), jnp.zeros((dh + ONES_ROWS, tq), F32))
        scores(0, s_a, mx_a)
        carry = lax.fori_loop(0, n_chunks // 2 - 1, body, (init,) * group)
        scores(n_chunks - 1, s_b, mx_b)
        carry = update(n_chunks - 2, s_a, mx_a, carry)
        carry = update(n_chunks - 1, s_b, mx_b, carry)
        for pair in range(group // 2):
            ot = jnp.concatenate([acc[:dh] / acc[dh:dh + 1] for (_, acc) in carry[2 * pair:2 * pair + 2]],
                                 axis=0)
            col = (g * group // 2 + pair) * LANES
            o_ref[:, col:col + LANES] = ot.T.astype(BF16)


def _attention(qt, k, vt, batch, seq, tq):
    n_tok = k.shape[0]
    kc = vt.shape[2]
    nq = seq // tq
    n_chunks = seq // kc
    assert n_chunks % 2 == 0, "the key loop handles chunks in pairs"
    group = N_HEADS_A // N_KV_HEADS_A
    return pl.pallas_call(
        functools.partial(_attn_kernel, n_chunks=n_chunks, kc=kc),
        grid=(batch, nq),
        in_specs=[pl.BlockSpec((D_A, tq), lambda b, i: (0, b * nq + i)),
                  pl.BlockSpec((seq, D_KV_A), lambda b, i: (b, 0)),
                  pl.BlockSpec((n_chunks, D_KV_A, kc), lambda b, i: (b, 0, 0))],
        out_specs=pl.BlockSpec((tq, D_A), lambda b, i: (b * nq + i, 0)),
        out_shape=jax.ShapeDtypeStruct((n_tok, D_A), BF16),
        scratch_shapes=[pltpu.VMEM((group, kc, tq), F32), pltpu.VMEM((group, kc, tq), F32),
                        pltpu.VMEM((group, 1, tq), F32), pltpu.VMEM((group, 1, tq), F32)],
        compiler_params=_params(2),
        name="attention",
    )(qt, k, vt)


def _log_sigmoid(x):
    return jnp.minimum(x, 0.0) - jnp.log1p(jnp.exp(-jnp.abs(x)))


def _split3(x):
    hi = x.astype(BF16)
    r1 = x - hi.astype(F32)
    mid = r1.astype(BF16)
    lo = (r1 - mid.astype(F32)).astype(BF16)
    return hi, mid, lo


GATE_ROWS = 2 * N_GATES_B


def _gate_prep_kernel(g_ref, bias_ref, bc_out, w0_out, rows_out, stat_out):
    tg = g_ref.shape[0]
    L = CHUNK_B
    n_chunk = tg // L
    gcol = g_ref[...] + bias_ref[...]
    pieces = _split3(_log_sigmoid(gcol))
    r = lax.broadcasted_iota(jnp.int32, (tg, tg), 0)
    c = lax.broadcasted_iota(jnp.int32, (tg, tg), 1)
    log2_l = L.bit_length() - 1
    same = lax.shift_right_logical(r, log2_l) == lax.shift_right_logical(c, log2_l)

    def chunk_sums(msk):
        sel = jnp.where(msk, 1.0, 0.0).astype(BF16)
        return sum(_dot(sel, piece) for piece in pieces)

    pre = chunk_sums(jnp.logical_and(same, c <= r))
    suf = chunk_sums(jnp.logical_and(same, c >= r))
    tot = chunk_sums(same)
    fwd_lane = lax.broadcasted_iota(jnp.int32, (tg, LANES), 1) < 2 * N_HEADS_B
    bc = jnp.where(fwd_lane, pre, suf)
    to_input_lanes = lambda x: pltpu.roll(x, LANES - N_HEADS_B, 1)
    tot_i = to_input_lanes(tot)
    w_log = (tot_i - to_input_lanes(bc) + gcol).reshape(n_chunk, L, LANES)
    wmax = jnp.max(w_log, axis=1, keepdims=True)
    bc_out[...] = bc
    w0_out[...] = jnp.exp(w_log - wmax).reshape(tg, LANES)
    rows_out[0:N_GATES_B, :] = gcol.T[0:N_GATES_B]
    rows_out[N_GATES_B:GATE_ROWS, :] = bc.T[0:N_GATES_B]
    stat_out[:, 0:1, :] = tot_i.reshape(n_chunk, L, LANES)[:, 0:1, :]
    stat_out[:, 1:2, :] = wmax
    stat_out[:, 2:SUBLANES, :] = jnp.zeros((n_chunk, SUBLANES - 2, LANES), F32)


def _gate_prep(g, bias_row, tg):
    n_tok = g.shape[0]
    per = tg // CHUNK_B
    tok = pl.BlockSpec((tg, LANES), lambda i: (i, 0))
    return pl.pallas_call(
        _gate_prep_kernel,
        grid=(n_tok // tg,),
        in_specs=[tok, pl.BlockSpec((1, LANES), lambda i: (0, 0))],
        out_specs=(tok, tok, pl.BlockSpec((GATE_ROWS, tg), lambda i: (0, i)),
                   pl.BlockSpec((per, SUBLANES, LANES), lambda i: (i, 0, 0))),
        out_shape=(jax.ShapeDtypeStruct((n_tok, LANES), F32), jax.ShapeDtypeStruct((n_tok, LANES), F32),
                   jax.ShapeDtypeStruct((GATE_ROWS, n_tok), F32),
                   jax.ShapeDtypeStruct((n_tok // CHUNK_B, SUBLANES, LANES), F32)),
        compiler_params=_params(1),
        name="gate_prep",
    )(g, bias_row)


def _mlstm_kernel(qf_ref, ktf_ref, vf_ref, bcf_ref, w0f_ref, grf_ref, stf_ref,
                  qb_ref, ktb_ref, vb_ref, bcb_ref, w0b_ref, grb_ref, stb_ref,
                  hf_out, hb_out, ct_scr, m_scr, *, cps):
    j = pl.program_id(1)
    L = CHUNK_B
    dh = HEAD_DIM_B

    @pl.when(j == 0)
    def _():
        ct_scr[...] = jnp.zeros_like(ct_scr)
        m_scr[...] = jnp.zeros_like(m_scr)

    row = lax.broadcasted_iota(jnp.int32, (L, L), 0)
    col = lax.broadcasted_iota(jnp.int32, (L, L), 1)
    lower = col <= row
    upper = col >= row
    ones = jnp.ones((L, dh), BF16)
    sel_row = lax.broadcasted_iota(jnp.int32, (LANES, LANES), 0)

    dirs = ((qf_ref, ktf_ref, vf_ref, bcf_ref, w0f_ref, grf_ref, stf_ref, hf_out),
            (qb_ref, ktb_ref, vb_ref, bcb_ref, w0b_ref, grb_ref, stb_ref, hb_out))
    units = []
    for cc in range(cps):
        for d, (q_ref, kt_ref, v_ref, bc_ref, w0_ref, gr_ref, st_ref, out_ref) in enumerate(dirs):
            ci = cc if d == 0 else cps - 1 - cc
            rows = slice(ci * L, (ci + 1) * L)
            bc_pieces = _split3(bc_ref[rows, :])
            w0_all = w0_ref[rows, :].astype(BF16)
            stat_pieces = _split3(st_ref[ci])
            for hd in range(N_HEADS_B):
                u = dict(unit=d * N_HEADS_B + hd, mask=lower if d == 0 else upper, out_ref=out_ref, rows=rows,
                         lanes=slice(hd * dh, (hd + 1) * dh))
                ci_col = d * 2 * N_HEADS_B + hd
                cf_col = ci_col + N_HEADS_B
                pick_f = jnp.where(sel_row == cf_col, 1.0, 0.0).astype(BF16)
                pick_i = jnp.where(sel_row == ci_col, 1.0, 0.0).astype(BF16)
                u["b_c"] = sum(_dot(piece, pick_f) for piece in bc_pieces)
                stat = sum(_dot(piece, pick_i) for piece in stat_pieces)
                u["b_last"], u["wmax"] = stat[0:1], stat[1:2]
                u["w0"] = _dot(w0_all, pick_i)
                u["i_r"] = gr_ref[ci_col:ci_col + 1, rows]
                u["b_r"] = gr_ref[N_GATES_B + cf_col:N_GATES_B + cf_col + 1, rows]
                u["q"] = q_ref[rows, u["lanes"]]
                u["kt"] = kt_ref[u["lanes"], rows]
                u["v_aug"] = jnp.concatenate([v_ref[rows, u["lanes"]], ones], axis=1)
                u["qk"] = _dot(u["q"], u["kt"])
                units.append(u)

    for u in units:
        log_d = jnp.where(u["mask"], u["b_c"] - u["b_r"] + u["i_r"], -jnp.inf)
        u["a"] = jnp.max(log_d, axis=-1, keepdims=True)
        u["p"] = (u["qk"] * jnp.exp(log_d - u["a"])).astype(BF16)
        u["wv"] = (jnp.concatenate([u["w0"], u["w0"]], axis=1) * u["v_aug"].astype(F32)).astype(BF16)
    for u in units:
        u["intra"] = _dot(u["p"], u["v_aug"])
        u["upd"] = _dot(u["kt"], u["wv"])

    for u in units:
        m = m_scr[u["unit"]]
        ct = ct_scr[u["unit"]]
        inter = u["b_c"] + m
        m_row = jnp.maximum(u["a"], inter)
        w_intra = jnp.exp(u["a"] - m_row)
        w_inter = jnp.exp(inter - m_row)
        cross = _dot(u["q"], ct.astype(BF16))
        den = w_inter * cross[:, dh:] + w_intra * u["intra"][:, dh:]
        inv = 1.0 / jnp.maximum(jnp.abs(den), jnp.exp(-m_row))
        u["out_ref"][u["rows"], u["lanes"]] = ((w_inter * inv) * cross[:, :dh]
                                               + (w_intra * inv) * u["intra"][:, :dh])
        m_new = jnp.maximum(u["b_last"] + m, u["wmax"])
        decay = jnp.exp(u["b_last"] + m - m_new)
        gain = jnp.exp(u["wmax"] - m_new)
        ct_scr[u["unit"]] = (jnp.concatenate([decay, decay], axis=1) * ct
                             + jnp.concatenate([gain, gain], axis=1) * u["upd"])
        m_scr[u["unit"]] = m_new


def _mlstm(qb, kbt, vb, bc, w0, grows, stat, batch, seq, cps):
    n_tok = qb.shape[0]
    rows = cps * CHUNK_B
    ns = seq // rows
    fwd = lambda b, j: b * ns + j
    bwd = lambda b, j: b * ns + ns - 1 - j

    def specs(blk):
        return [pl.BlockSpec((rows, D_B), lambda b, j: (blk(b, j), 0)),
                pl.BlockSpec((D_B, rows), lambda b, j: (0, blk(b, j))),
                pl.BlockSpec((rows, D_B), lambda b, j: (blk(b, j), 0)),
                pl.BlockSpec((rows, LANES), lambda b, j: (blk(b, j), 0)),
                pl.BlockSpec((rows, LANES), lambda b, j: (blk(b, j), 0)),
                pl.BlockSpec((GATE_ROWS, rows), lambda b, j: (0, blk(b, j))),
                pl.BlockSpec((cps, SUBLANES, LANES), lambda b, j: (blk(b, j), 0, 0))]

    out = lambda blk: pl.BlockSpec((rows, D_B), lambda b, j: (blk(b, j), 0))
    n_units = 2 * N_HEADS_B
    args = (qb, kbt, vb, bc, w0, grows, stat)
    return pl.pallas_call(
        functools.partial(_mlstm_kernel, cps=cps),
        grid=(batch, ns),
        in_specs=specs(fwd) + specs(bwd),
        out_specs=(out(fwd), out(bwd)),
        out_shape=(jax.ShapeDtypeStruct((n_tok, D_B), F32), jax.ShapeDtypeStruct((n_tok, D_B), F32)),
        scratch_shapes=[pltpu.VMEM((n_units, HEAD_DIM_B, 2 * HEAD_DIM_B), F32),
                        pltpu.VMEM((n_units, 1, LANES), F32)],
        compiler_params=_params(2),
        name="mlstm",
    )(*args, *args)


def _out_proj_kernel(x_ref, oa_ref, hf_ref, hb_ref, ob_ref, hg_ref, w_ref, o_ref):
    hsum = hf_ref[...] + hb_ref[...]
    parts = []
    for hd in range(N_HEADS_B):
        hh = hsum[:, hd * HEAD_DIM_B:(hd + 1) * HEAD_DIM_B]
        parts.append(hh * lax.rsqrt(jnp.mean(hh * hh, axis=-1, keepdims=True) + EPS))
    hn = jnp.concatenate(parts, axis=1) * hg_ref[...]
    out_b = (hn * _sigmoid(ob_ref[...])).astype(BF16)
    o_ref[...] = x_ref[...] + _dot(oa_ref[...], w_ref[:D_A]) + _dot(out_b, w_ref[D_A:])


def _out_proj(x2d, out_a, hf, hb, ob, hg, w, tm):
    n_tok = x2d.shape[0]
    tok = lambda d: pl.BlockSpec((tm, d), lambda i: (i, 0))
    return pl.pallas_call(
        _out_proj_kernel,
        grid=(n_tok // tm,),
        in_specs=[tok(D_MODEL), tok(D_A), tok(D_B), tok(D_B), tok(D_B),
                  pl.BlockSpec((1, D_B), lambda i: (0, 0)),
                  pl.BlockSpec((D_A + D_B, D_MODEL), lambda i: (0, 0))],
        out_specs=tok(D_MODEL),
        out_shape=jax.ShapeDtypeStruct((n_tok, D_MODEL), F32),
        compiler_params=_params(1),
        name="out_proj",
    )(x2d, out_a, hf, hb, ob, hg, w)


MXU_COLS = 256


def _ffn_kernel(xp_ref, x_ref, xn_ref, gain_ref, wu_ref, cw_ref, cb_ref, wd_ref, o_ref, *, tiles_per_seq):
    i = pl.program_id(0)
    tm = x_ref.shape[0]
    first = (i % tiles_per_seq) == 0
    last = (i % tiles_per_seq) == tiles_per_seq - 1
    hext = _normed_window(xp_ref, x_ref, xn_ref, gain_ref[...], first, last)
    acc = x_ref[...]
    for lo, hi in ((0, 5 * MXU_COLS), (5 * MXU_COLS, D_FF)):
        gcols = slice(lo, hi)
        vcols = slice(D_FF + lo, D_FF + hi)
        gate = _conv3(_dot(hext, wu_ref[:, gcols]), cw_ref[:, gcols], tm) + cb_ref[:, gcols]
        val = _conv3(_dot(hext, wu_ref[:, vcols]), cw_ref[:, vcols], tm) + cb_ref[:, vcols]
        acc = acc + _dot((gate * _sigmoid(gate) * val).astype(BF16), wd_ref[lo:hi, :])
    o_ref[...] = acc


def _ffn(x2d, seq, tm, gain, w_up, cw, cb, w_down):
    n_tok = x2d.shape[0]
    prev, main, nxt = _halo_specs(tm, D_MODEL, n_tok)
    full = lambda r: pl.BlockSpec((r, 2 * D_FF), lambda i: (0, 0))
    once = pl.Buffered(1)
    return pl.pallas_call(
        functools.partial(_ffn_kernel, tiles_per_seq=seq // tm),
        grid=(n_tok // tm,),
        in_specs=[prev, main, nxt, pl.BlockSpec((1, D_MODEL), lambda i: (0, 0)),
                  pl.BlockSpec((D_MODEL, 2 * D_FF), lambda i: (0, 0), pipeline_mode=once), full(3), full(1),
                  pl.BlockSpec((D_FF, D_MODEL), lambda i: (0, 0), pipeline_mode=once)],
        out_specs=pl.BlockSpec((tm, D_MODEL), lambda i: (i, 0)),
        out_shape=jax.ShapeDtypeStruct((n_tok, D_MODEL), F32),
        compiler_params=_params(1),
        name="ffn",
    )(x2d, x2d, x2d, gain, w_up, cw, cb, w_down)


def _conf_glu_kernel(x_ref, gain_ref, w_ref, b_ref, o_ref):
    h = _rms(x_ref[...], gain_ref[...]).astype(BF16)
    u = _dot(h, w_ref[...]) + b_ref[...]
    o_ref[...] = u[:, :D_MODEL] * _sigmoid(u[:, D_MODEL:])


def _conf_glu(x2d, gain, w, b, tm):
    n_tok = x2d.shape[0]
    return pl.pallas_call(
        _conf_glu_kernel,
        grid=(n_tok // tm,),
        in_specs=[pl.BlockSpec((tm, D_MODEL), lambda i: (i, 0)), pl.BlockSpec((1, D_MODEL), lambda i: (0, 0)),
                  pl.BlockSpec((D_MODEL, 2 * D_MODEL), lambda i: (0, 0)),
                  pl.BlockSpec((1, 2 * D_MODEL), lambda i: (0, 0))],
        out_specs=pl.BlockSpec((tm, D_MODEL), lambda i: (i, 0)),
        out_shape=jax.ShapeDtypeStruct((n_tok, D_MODEL), F32),
        compiler_params=_params(1),
        name="conf_glu",
    )(x2d, gain, w, b)


CONV_ROWS = 64


def _conf_conv_kernel(up_ref, u_ref, un_ref, x_ref, wdw_ref, bdw_ref, lng_ref, lnb_ref, w2_ref, b2_ref, o_ref,
                      rot_scr, act_scr, *, tiles_per_seq):
    i = pl.program_id(0)
    tm = u_ref.shape[0]
    n = tm + 2 * HALO
    first = (i % tiles_per_seq) == 0
    last = (i % tiles_per_seq) == tiles_per_seq - 1
    win = jnp.concatenate([jnp.where(first, 0.0, up_ref[...]), u_ref[...], jnp.where(last, 0.0, un_ref[...])], axis=0)
    rot_scr[0] = win
    for r in range(1, 8):
        rot_scr[r] = pltpu.roll(win, n - r, 0)

    for c in range(D_MODEL // LANES):
        lanes = slice(c * LANES, (c + 1) * LANES)
        taps = [wdw_ref[k, :, lanes] for k in range(CONV_C)]

        def block(rb, carry, lanes=lanes, taps=taps):
            r0 = pl.multiple_of(rb * CONV_ROWS, CONV_ROWS)
            for a in range(CONV_ROWS // SUBLANES):
                acc = None
                for k in range(CONV_C):
                    shift = k + 1
                    rows = pl.ds(r0 + (shift // SUBLANES + a) * SUBLANES, SUBLANES)
                    term = taps[k] * rot_scr[shift % SUBLANES, rows, lanes]
                    acc = term if acc is None else acc + term
                act_scr[pl.ds(r0 + a * SUBLANES, SUBLANES), lanes] = acc
            return carry

        lax.fori_loop(0, tm // CONV_ROWS, block, 0)
    conv = act_scr[...] + bdw_ref[...]
    xc = conv - jnp.mean(conv, axis=-1, keepdims=True)
    y = xc * lax.rsqrt(jnp.mean(xc * xc, axis=-1, keepdims=True) + EPS) * lng_ref[...] + lnb_ref[...]
    act = (y * _sigmoid(y)).astype(BF16)
    o_ref[...] = x_ref[...] + _dot(act, w2_ref[...]) + b2_ref[...]


def _conf_conv(u, x2d, seq, tm, wdw, bdw, lng, lnb, w2, b2):
    n_tok = x2d.shape[0]
    prev, main, nxt = _halo_specs(tm, D_MODEL, n_tok)
    vec = pl.BlockSpec((1, D_MODEL), lambda i: (0, 0))
    return pl.pallas_call(
        functools.partial(_conf_conv_kernel, tiles_per_seq=seq // tm),
        grid=(n_tok // tm,),
        in_specs=[prev, main, nxt, pl.BlockSpec((tm, D_MODEL), lambda i: (i, 0)),
                  pl.BlockSpec((CONV_C, SUBLANES, D_MODEL), lambda i: (0, 0, 0)), vec, vec, vec,
                  pl.BlockSpec((D_MODEL, D_MODEL), lambda i: (0, 0)), vec],
        out_specs=pl.BlockSpec((tm, D_MODEL), lambda i: (i, 0)),
        out_shape=jax.ShapeDtypeStruct((n_tok, D_MODEL), F32),
        scratch_shapes=[pltpu.VMEM((8, tm + 2 * HALO, D_MODEL), F32), pltpu.VMEM((tm, D_MODEL), F32)],
        compiler_params=_params(1),
        name="conf_conv",
    )(u, u, u, x2d, wdw, bdw, lng, lnb, w2, b2)


def _rope_tables(seq):
    pos = jnp.arange(seq)
    inv = ROPE_THETA ** (-jnp.arange(ROPE_PAIRS, dtype=F32) / ROPE_PAIRS)
    lane = np.arange(HEAD_DIM_A)
    section, half, pair = lane // (2 * ROPE_PAIRS), (lane // ROPE_PAIRS) % 2, lane % ROPE_PAIRS
    row_idx = (pos // GRID_W).astype(F32)[:, None]
    col_idx = (pos % GRID_W).astype(F32)[:, None]
    ang = jnp.where(jnp.asarray(section == 0)[None, :], row_idx, col_idx) * inv[pair][None, :]
    sign = jnp.asarray(np.where(half == 0, -1.0, 1.0), F32)[None, :]
    return jnp.tile(jnp.cos(ang), (1, 2)), jnp.tile(jnp.sin(ang) * sign, (1, 2))


def _segment_mean_matrix(n, width):
    seg = np.arange(n) // width
    return jnp.asarray((seg[:, None] == seg[None, :]).astype(np.float32) / width, BF16)


def _tile_size(seq, want):
    return min(want, seq)


def _trunk(x, p):
    batch, seq, _ = x.shape
    n_tok = batch * seq
    x2d = x.reshape(n_tok, D_MODEL)
    tm = _tile_size(seq, 512)
    cps = 2

    qt, k, vt, qb, kbt, vb, ob, g = _in_proj(x2d, seq, tm, p["mix_norm_e"], p["w_in"], p["cos"], p["sin"],
                                            p["segq"], p["segk"], p["qg"], p["kg"], p["w_qk_conv"], p["w_v_t"])
    out_a = _attention(qt, k, vt, batch, seq, _tile_size(seq, 512))
    bc, w0, grows, stat = _gate_prep(g, p["b_gates_row"], tm)
    hf, hb = _mlstm(qb, kbt, vb, bc, w0, grows, stat, batch, seq, cps)
    x2d = _out_proj(x2d, out_a, hf, hb, ob, p["h_gain"], p["w_out"], tm)
    x2d = _ffn(x2d, seq, tm, p["ffn_norm0"], p["w_up0"], p["w_dw_ff0"], p["b_dw_ff0"], p["w_down0"])
    u = _conf_glu(x2d, p["mix_norm_o"], p["w_pw1"], p["b_pw1"], tm)
    x2d = _conf_conv(u, x2d, seq, tm, p["w_dw_c"], p["b_dw_c"], p["ln_g"], p["ln_b"],
                     p["w_pw2"], p["b_pw2"])
    x2d = _ffn(x2d, seq, tm, p["ffn_norm1"], p["w_up1"], p["w_dw_ff1"], p["b_dw_ff1"], p["w_down1"])
    return x2d.reshape(batch, seq, D_MODEL)


def kernel(x_prompt, x_sample, mix_norm_e, w_in, q_gain_a, k_gain_a, w_qk_conv_b, b_gates_b, h_gain_b, w_out_e, mix_norm_o, w_pw1_c, b_pw1_c, w_dw_c, b_dw_c, ln_g_c, ln_b_c, w_pw2_c, b_pw2_c, ffn_norm, w_up, w_dw_ff, b_dw_ff, w_down):
    row = lambda a: a.reshape(1, -1).astype(F32)
    p = {
        "mix_norm_e": row(mix_norm_e[0]),
        "w_in": jnp.pad(w_in[0].astype(BF16), ((0, 0), (0, IN_COLS_PAD - IN_COLS))),
        "segq": _segment_mean_matrix(D_A, HEAD_DIM_A),
        "segk": _segment_mean_matrix(D_KV_A, HEAD_DIM_A),
        "qg": row(jnp.tile(q_gain_a[0], N_HEADS_A)),
        "kg": row(jnp.tile(k_gain_a[0], N_KV_HEADS_A)),
        "w_qk_conv": w_qk_conv_b[0].astype(F32),
        "w_v_t": w_in[0][:, _O_VA:_O_QKB].T.astype(BF16),
        "b_gates_row": jnp.pad(row(b_gates_b[0]), ((0, 0), (0, LANES - N_GATES_B))),
        "h_gain": row(h_gain_b[0]),
        "w_out": w_out_e[0].astype(BF16),
        "mix_norm_o": row(mix_norm_o[0]),
        "w_pw1": w_pw1_c[0].astype(BF16),
        "b_pw1": row(b_pw1_c[0]),
        "w_dw_c": jnp.broadcast_to(w_dw_c[0].astype(F32)[:, None, :], (CONV_C, SUBLANES, D_MODEL)),
        "b_dw_c": row(b_dw_c[0]),
        "ln_g": row(ln_g_c[0]),
        "ln_b": row(ln_b_c[0]),
        "w_pw2": w_pw2_c[0].astype(BF16),
        "b_pw2": row(b_pw2_c[0]),
    }
    for layer in range(2):
        p[f"ffn_norm{layer}"] = row(ffn_norm[layer])
        p[f"w_up{layer}"] = w_up[layer].astype(BF16)
        p[f"w_dw_ff{layer}"] = w_dw_ff[layer].astype(F32)
        p[f"b_dw_ff{layer}"] = row(b_dw_ff[layer])
        p[f"w_down{layer}"] = w_down[layer].astype(BF16)
    outs = []
    for x in (x_prompt, x_sample):
        p["cos"], p["sin"] = _rope_tables(x.shape[1])
        outs.append(_trunk(x, p))
    return tuple(outs)
```

```python
import functools

import numpy as np
import jax
import jax.numpy as jnp
from jax import lax
from jax.experimental import pallas as pl
from jax.experimental.pallas import tpu as pltpu

D_MODEL = 1024
GRID_W = 64
N_HEADS_A = 8
N_KV_HEADS_A = 2
HEAD_DIM_A = 64
D_A = N_HEADS_A * HEAD_DIM_A
D_KV_A = N_KV_HEADS_A * HEAD_DIM_A
ROPE_THETA = 10000.0
ROPE_PAIRS = HEAD_DIM_A // 4
N_HEADS_B = 4
HEAD_DIM_B = 128
D_B = N_HEADS_B * HEAD_DIM_B
CHUNK_B = 128
N_GATES_B = 4 * N_HEADS_B
IN_COLS = D_A + 2 * D_KV_A + 4 * D_B + N_GATES_B
IN_COLS_PAD = 2944
CONV_C = 31
D_FF = 2816
EPS = 1e-6
Q_SCALE = HEAD_DIM_A ** -0.5 * float(np.log2(np.e))

LANES = 128
SUBLANES = 8
HALO = 16
ONES_ROWS = 16
VMEM_LIMIT = 56 * 1024 * 1024

F32 = jnp.float32
BF16 = jnp.bfloat16

_O_QA, _O_KA, _O_VA, _O_QKB, _O_VB, _O_OB, _O_G = 0, 512, 640, 768, 1792, 2304, 2816


def _params(n_axes):
    return pltpu.CompilerParams(dimension_semantics=("arbitrary",) * n_axes,
                                vmem_limit_bytes=VMEM_LIMIT)


def _dot(a, b):
    return jnp.dot(a, b, preferred_element_type=F32)


def _rms(x, gain):
    ms = jnp.mean(x * x, axis=-1, keepdims=True)
    return x * lax.rsqrt(ms + EPS) * gain


def _sigmoid(x):
    return 1.0 / (1.0 + jnp.exp(-x))


def _normed_window(xp_ref, x_ref, xn_ref, gain, first, last):
    hp = jnp.where(first, 0.0, _rms(xp_ref[...], gain))
    hn = jnp.where(last, 0.0, _rms(xn_ref[...], gain))
    h = _rms(x_ref[...], gain)
    return jnp.concatenate([hp, h, hn], axis=0).astype(BF16)


def _conv3(u, cw, tm):
    n = u.shape[0]
    um = pltpu.roll(u, 1, 0)[HALO:HALO + tm]
    uc = u[HALO:HALO + tm]
    up = pltpu.roll(u, n - 1, 0)[HALO:HALO + tm]
    return um * cw[0:1] + uc * cw[1:2] + up * cw[2:3]


def _halo_specs(tm, d, n_tokens, axis=0, n_axes=1):
    r = tm // HALO
    last_blk = n_tokens // HALO - 1

    def pick(idx):
        return idx[axis]

    prev = pl.BlockSpec((HALO, d), lambda *idx: (jnp.maximum(pick(idx) * r - 1, 0), 0))
    main = pl.BlockSpec((tm, d), lambda *idx: (pick(idx), 0))
    nxt = pl.BlockSpec((HALO, d), lambda *idx: (jnp.minimum((pick(idx) + 1) * r, last_blk), 0))
    return prev, main, nxt


def _rope(xn, cos, sin, width):
    lane = lax.broadcasted_iota(jnp.int32, xn.shape, 1)
    first_half = (lane % (2 * ROPE_PAIRS)) < ROPE_PAIRS
    partner = jnp.where(first_half, pltpu.roll(xn, width - ROPE_PAIRS, 1), pltpu.roll(xn, ROPE_PAIRS, 1))
    return xn * cos + partner * sin


def _in_proj_kernel(xp_ref, x_ref, xn_ref, gain_ref, w_ref, cos_ref, sin_ref, segq_ref, segk_ref,
                    qg_ref, kg_ref, cw_ref, wvt_ref,
                    qt_out, k_out, vt_out, qb_out, kbt_out, vb_out, ob_out, g_out, *, tiles_per_seq):
    i = pl.program_id(0)
    tm = x_ref.shape[0]
    first = (i % tiles_per_seq) == 0
    last = (i % tiles_per_seq) == tiles_per_seq - 1
    hext = _normed_window(xp_ref, x_ref, xn_ref, gain_ref[...], first, last)
    h = hext[HALO:HALO + tm]

    cos2 = cos_ref[...]
    sin2 = sin_ref[...]
    qa = _dot(h, w_ref[:, _O_QA:_O_KA])
    ka = _dot(h, w_ref[:, _O_KA:_O_VA])
    ms = _dot((qa * qa).astype(BF16), segq_ref[...])
    msk = _dot((ka * ka).astype(BF16), segk_ref[...])
    u = _dot(hext, w_ref[:, _O_QKB:_O_VB])

    qn = qa * lax.rsqrt(ms + EPS) * qg_ref[...]
    cos = jnp.concatenate([cos2] * (D_A // LANES), axis=1)
    sin = jnp.concatenate([sin2] * (D_A // LANES), axis=1)
    qt_out[...] = (_rope(qn, cos, sin, D_A) * Q_SCALE).T.astype(BF16)
    kn = ka * lax.rsqrt(msk + EPS) * kg_ref[...]
    k_out[...] = _rope(kn, cos2, sin2, D_KV_A).astype(BF16)

    c = _conv3(u, cw_ref[...], tm)
    act = c * _sigmoid(c)
    qb_out[...] = act[:, :D_B].astype(BF16)
    kbt_out[...] = (act[:, D_B:] * (HEAD_DIM_B ** -0.5)).T.astype(BF16)

    vb_out[...] = _dot(h, w_ref[:, _O_VB:_O_OB]).astype(BF16)
    ob_out[...] = _dot(h, w_ref[:, _O_OB:_O_G])
    g_out[...] = _dot(h, w_ref[:, _O_G:IN_COLS_PAD])
    vt = lax.dot_general(wvt_ref[...], h, (((1,), (1,)), ((), ())), preferred_element_type=F32)
    vt_out[0] = vt.astype(BF16)


def _in_proj(x2d, seq, tm, gain, w_pad, cos, sin, segq, segk, qg, kg, cw, wvt):
    n_tok = x2d.shape[0]
    nt = n_tok // tm
    tps = seq // tm
    prev, main, nxt = _halo_specs(tm, D_MODEL, n_tok)
    const = lambda shape: pl.BlockSpec(shape, lambda i: (0,) * len(shape))
    rope_spec = pl.BlockSpec((tm, LANES), lambda i: (i % tps, 0))
    tok = lambda d: pl.BlockSpec((tm, d), lambda i: (i, 0))
    out_shape = (
        jax.ShapeDtypeStruct((D_A, n_tok), BF16),
        jax.ShapeDtypeStruct((n_tok, D_KV_A), BF16),
        jax.ShapeDtypeStruct((nt, D_KV_A, tm), BF16),
        jax.ShapeDtypeStruct((n_tok, D_B), BF16),
        jax.ShapeDtypeStruct((D_B, n_tok), BF16),
        jax.ShapeDtypeStruct((n_tok, D_B), BF16),
        jax.ShapeDtypeStruct((n_tok, D_B), F32),
        jax.ShapeDtypeStruct((n_tok, LANES), F32),
    )
    out_specs = (pl.BlockSpec((D_A, tm), lambda i: (0, i)), tok(D_KV_A),
                 pl.BlockSpec((1, D_KV_A, tm), lambda i: (i, 0, 0)),
                 tok(D_B), pl.BlockSpec((D_B, tm), lambda i: (0, i)), tok(D_B), tok(D_B), tok(LANES))
    return pl.pallas_call(
        functools.partial(_in_proj_kernel, tiles_per_seq=tps),
        grid=(nt,),
        in_specs=[prev, main, nxt, const((1, D_MODEL)), const((D_MODEL, IN_COLS_PAD)), rope_spec, rope_spec,
                  const((D_A, D_A)), const((D_KV_A, D_KV_A)), const((1, D_A)), const((1, D_KV_A)),
                  const((3, 2 * D_B)), const((D_KV_A, D_MODEL))],
        out_specs=out_specs,
        out_shape=out_shape,
        compiler_params=_params(1),
        name="in_proj",
    )(x2d, x2d, x2d, gain, w_pad, cos, sin, segq, segk, qg, kg, cw, wvt)


def _attn_kernel(qt_ref, k_ref, vt_ref, o_ref, s_a, s_b, mx_a, mx_b, *, n_chunks, kc):
    tq = qt_ref.shape[1]
    dh = HEAD_DIM_A
    group = N_HEADS_A // N_KV_HEADS_A
    zeros = jnp.zeros((dh, tq), BF16)
    ones = jnp.ones((ONES_ROWS, kc), BF16)
    for g in range(N_KV_HEADS_A):
        rhs = []
        for h in range(g * group, (g + 1) * group):
            qh = qt_ref[h * dh:(h + 1) * dh, :]
            rhs.append(jnp.concatenate([qh, zeros] if g == 0 else [zeros, qh], axis=0))

        def score1(c, j, s_scr, mx_scr, rhs=rhs):
            kblk = k_ref[pl.ds(pl.multiple_of(c * kc, kc), kc), :]
            s = _dot(kblk, rhs[j])
            s_scr[j] = s
            mx_scr[j] = jnp.max(s, axis=0, keepdims=True)

        def update1(c, j, s_scr, mx_scr, state, g=g):
            vblk = jnp.concatenate([vt_ref[c, g * dh:(g + 1) * dh, :], ones], axis=0)
            m, acc = state
            m_new = jnp.maximum(m, mx_scr[j])
            p = jnp.exp2(s_scr[j] - m_new)
            acc = jnp.exp2(m - m_new) * acc + _dot(vblk, p.astype(BF16))
            return m_new, acc

        def scores(c, s_scr, mx_scr, score1=score1):
            for j in range(group):
                score1(c, j, s_scr, mx_scr)

        def update(c, s_scr, mx_scr, carry, update1=update1):
            return tuple(update1(c, j, s_scr, mx_scr, state) for j, state in enumerate(carry))

        def fused(c_next, s_next, mx_next, c, s_cur, mx_cur, carry, score1=score1, update1=update1):
            out = []
            for j, state in enumerate(carry):
                score1(c_next, j, s_next, mx_next)
                out.append(update1(c, j, s_cur, mx_cur, state))
            return tuple(out)

        def body(i, carry, fused=fused):
            c = 2 * i
            carry = fused(c + 1, s_b, mx_b, c, s_a, mx_a, carry)
            return fused(c + 2, s_a, mx_a, c + 1, s_b, mx_b, carry)

        init = (jnp.full((1, tq), -jnp.inf, F32), jnp.zeros((dh + ONES_ROWS, tq), F32))
        scores(0, s_a, mx_a)
        carry = lax.fori_loop(0, n_chunks // 2 - 1, body, (init,) * group)
        scores(n_chunks - 1, s_b, mx_b)
        carry = update(n_chunks - 2, s_a, mx_a, carry)
        carry = update(n_chunks - 1, s_b, mx_b, carry)
        for pair in range(group // 2):
            ot = jnp.concatenate([acc[:dh] / acc[dh:dh + 1] for (_, acc) in carry[2 * pair:2 * pair + 2]],
                                 axis=0)
            col = (g * group // 2 + pair) * LANES
            o_ref[:, col:col + LANES] = ot.T.astype(BF16)


def _attention(qt, k, vt, batch, seq, tq):
    n_tok = k.shape[0]
    kc = vt.shape[2]
    nq = seq // tq
    n_chunks = seq // kc
    assert n_chunks % 2 == 0, "the key loop handles chunks in pairs"
    group = N_HEADS_A // N_KV_HEADS_A
    return pl.pallas_call(
        functools.partial(_attn_kernel, n_chunks=n_chunks, kc=kc),
        grid=(batch, nq),
        in_specs=[pl.BlockSpec((D_A, tq), lambda b, i: (0, b * nq + i)),
                  pl.BlockSpec((seq, D_KV_A), lambda b, i: (b, 0)),
                  pl.BlockSpec((n_chunks, D_KV_A, kc), lambda b, i: (b, 0, 0))],
        out_specs=pl.BlockSpec((tq, D_A), lambda b, i: (b * nq + i, 0)),
        out_shape=jax.ShapeDtypeStruct((n_tok, D_A), BF16),
        scratch_shapes=[pltpu.VMEM((group, kc, tq), F32), pltpu.VMEM((group, kc, tq), F32),
                        pltpu.VMEM((group, 1, tq), F32), pltpu.VMEM((group, 1, tq), F32)],
        compiler_params=_params(2),
        name="attention",
    )(qt, k, vt)


def _log_sigmoid(x):
    return jnp.minimum(x, 0.0) - jnp.log1p(jnp.exp(-jnp.abs(x)))


def _split3(x):
    hi = x.astype(BF16)
    r1 = x - hi.astype(F32)
    mid = r1.astype(BF16)
    lo = (r1 - mid.astype(F32)).astype(BF16)
    return hi, mid, lo


GATE_ROWS = 2 * N_GATES_B


def _gate_prep_kernel(g_ref, bias_ref, bc_out, w0_out, rows_out, stat_out):
    tg = g_ref.shape[0]
    L = CHUNK_B
    n_chunk = tg // L
    gcol = g_ref[...] + bias_ref[...]
    pieces = _split3(_log_sigmoid(gcol))
    r = lax.broadcasted_iota(jnp.int32, (tg, tg), 0)
    c = lax.broadcasted_iota(jnp.int32, (tg, tg), 1)
    log2_l = L.bit_length() - 1
    same = lax.shift_right_logical(r, log2_l) == lax.shift_right_logical(c, log2_l)

    def chunk_sums(msk):
        sel = jnp.where(msk, 1.0, 0.0).astype(BF16)
        return sum(_dot(sel, piece) for piece in pieces)

    pre = chunk_sums(jnp.logical_and(same, c <= r))
    suf = chunk_sums(jnp.logical_and(same, c >= r))
    tot = chunk_sums(same)
    fwd_lane = lax.broadcasted_iota(jnp.int32, (tg, LANES), 1) < 2 * N_HEADS_B
    bc = jnp.where(fwd_lane, pre, suf)
    to_input_lanes = lambda x: pltpu.roll(x, LANES - N_HEADS_B, 1)
    tot_i = to_input_lanes(tot)
    w_log = (tot_i - to_input_lanes(bc) + gcol).reshape(n_chunk, L, LANES)
    wmax = jnp.max(w_log, axis=1, keepdims=True)
    bc_out[...] = bc
    w0_out[...] = jnp.exp(w_log - wmax).reshape(tg, LANES)
    rows_out[0:N_GATES_B, :] = gcol.T[0:N_GATES_B]
    rows_out[N_GATES_B:GATE_ROWS, :] = bc.T[0:N_GATES_B]
    stat_out[:, 0:1, :] = tot_i.reshape(n_chunk, L, LANES)[:, 0:1, :]
    stat_out[:, 1:2, :] = wmax
    stat_out[:, 2:SUBLANES, :] = jnp.zeros((n_chunk, SUBLANES - 2, LANES), F32)


def _gate_prep(g, bias_row, tg):
    n_tok = g.shape[0]
    per = tg // CHUNK_B
    tok = pl.BlockSpec((tg, LANES), lambda i: (i, 0))
    return pl.pallas_call(
        _gate_prep_kernel,
        grid=(n_tok // tg,),
        in_specs=[tok, pl.BlockSpec((1, LANES), lambda i: (0, 0))],
        out_specs=(tok, tok, pl.BlockSpec((GATE_ROWS, tg), lambda i: (0, i)),
                   pl.BlockSpec((per, SUBLANES, LANES), lambda i: (i, 0, 0))),
        out_shape=(jax.ShapeDtypeStruct((n_tok, LANES), F32), jax.ShapeDtypeStruct((n_tok, LANES), F32),
                   jax.ShapeDtypeStruct((GATE_ROWS, n_tok), F32),
                   jax.ShapeDtypeStruct((n_tok // CHUNK_B, SUBLANES, LANES), F32)),
        compiler_params=_params(1),
        name="gate_prep",
    )(g, bias_row)


def _mlstm_kernel(qf_ref, ktf_ref, vf_ref, bcf_ref, w0f_ref, grf_ref, stf_ref,
                  qb_ref, ktb_ref, vb_ref, bcb_ref, w0b_ref, grb_ref, stb_ref,
                  hf_out, hb_out, ct_scr, m_scr, *, cps):
    j = pl.program_id(1)
    L = CHUNK_B
    dh = HEAD_DIM_B

    @pl.when(j == 0)
    def _():
        ct_scr[...] = jnp.zeros_like(ct_scr)
        m_scr[...] = jnp.zeros_like(m_scr)

    row = lax.broadcasted_iota(jnp.int32, (L, L), 0)
    col = lax.broadcasted_iota(jnp.int32, (L, L), 1)
    lower = col <= row
    upper = col >= row
    ones = jnp.ones((L, dh), BF16)
    sel_row = lax.broadcasted_iota(jnp.int32, (LANES, LANES), 0)

    dirs = ((qf_ref, ktf_ref, vf_ref, bcf_ref, w0f_ref, grf_ref, stf_ref, hf_out),
            (qb_ref, ktb_ref, vb_ref, bcb_ref, w0b_ref, grb_ref, stb_ref, hb_out))
    units = []
    for cc in range(cps):
        for d, (q_ref, kt_ref, v_ref, bc_ref, w0_ref, gr_ref, st_ref, out_ref) in enumerate(dirs):
            ci = cc if d == 0 else cps - 1 - cc
            rows = slice(ci * L, (ci + 1) * L)
            bc_pieces = _split3(bc_ref[rows, :])
            w0_all = w0_ref[rows, :].astype(BF16)
            stat_pieces = _split3(st_ref[ci])
            for hd in range(N_HEADS_B):
                u = dict(unit=d * N_HEADS_B + hd, mask=lower if d == 0 else upper, out_ref=out_ref, rows=rows,
                         lanes=slice(hd * dh, (hd + 1) * dh))
                ci_col = d * 2 * N_HEADS_B + hd
                cf_col = ci_col + N_HEADS_B
                pick_f = jnp.where(sel_row == cf_col, 1.0, 0.0).astype(BF16)
                pick_i = jnp.where(sel_row == ci_col, 1.0, 0.0).astype(BF16)
                u["b_c"] = sum(_dot(piece, pick_f) for piece in bc_pieces)
                stat = sum(_dot(piece, pick_i) for piece in stat_pieces)
                u["b_last"], u["wmax"] = stat[0:1], stat[1:2]
                u["w0"] = _dot(w0_all, pick_i)
                u["i_r"] = gr_ref[ci_col:ci_col + 1, rows]
                u["b_r"] = gr_ref[N_GATES_B + cf_col:N_GATES_B + cf_col + 1, rows]
                u["q"] = q_ref[rows, u["lanes"]]
                u["kt"] = kt_ref[u["lanes"], rows]
                u["v_aug"] = jnp.concatenate([v_ref[rows, u["lanes"]], ones], axis=1)
                u["qk"] = _dot(u["q"], u["kt"])
                units.append(u)

    for u in units:
        log_d = jnp.where(u["mask"], u["b_c"] - u["b_r"] + u["i_r"], -jnp.inf)
        u["a"] = jnp.max(log_d, axis=-1, keepdims=True)
        u["p"] = (u["qk"] * jnp.exp(log_d - u["a"])).astype(BF16)
        u["wv"] = (jnp.concatenate([u["w0"], u["w0"]], axis=1) * u["v_aug"].astype(F32)).astype(BF16)
    for u in units:
        u["intra"] = _dot(u["p"], u["v_aug"])
        u["upd"] = _dot(u["kt"], u["wv"])

    for u in units:
        m = m_scr[u["unit"]]
        ct = ct_scr[u["unit"]]
        inter = u["b_c"] + m
        m_row = jnp.maximum(u["a"], inter)
        w_intra = jnp.exp(u["a"] - m_row)
        w_inter = jnp.exp(inter - m_row)
        cross = _dot(u["q"], ct.astype(BF16))
        den = w_inter * cross[:, dh:] + w_intra * u["intra"][:, dh:]
        inv = 1.0 / jnp.maximum(jnp.abs(den), jnp.exp(-m_row))
        u["out_ref"][u["rows"], u["lanes"]] = ((w_inter * inv) * cross[:, :dh]
                                               + (w_intra * inv) * u["intra"][:, :dh])
        m_new = jnp.maximum(u["b_last"] + m, u["wmax"])
        decay = jnp.exp(u["b_last"] + m - m_new)
        gain = jnp.exp(u["wmax"] - m_new)
        ct_scr[u["unit"]] = (jnp.concatenate([decay, decay], axis=1) * ct
                             + jnp.concatenate([gain, gain], axis=1) * u["upd"])
        m_scr[u["unit"]] = m_new


def _mlstm(qb, kbt, vb, bc, w0, grows, stat, batch, seq, cps):
    n_tok = qb.shape[0]
    rows = cps * CHUNK_B
    ns = seq // rows
    fwd = lambda b, j: b * ns + j
    bwd = lambda b, j: b * ns + ns - 1 - j

    def specs(blk):
        return [pl.BlockSpec((rows, D_B), lambda b, j: (blk(b, j), 0)),
                pl.BlockSpec((D_B, rows), lambda b, j: (0, blk(b, j))),
                pl.BlockSpec((rows, D_B), lambda b, j: (blk(b, j), 0)),
                pl.BlockSpec((rows, LANES), lambda b, j: (blk(b, j), 0)),
                pl.BlockSpec((rows, LANES), lambda b, j: (blk(b, j), 0)),
                pl.BlockSpec((GATE_ROWS, rows), lambda b, j: (0, blk(b, j))),
                pl.BlockSpec((cps, SUBLANES, LANES), lambda b, j: (blk(b, j), 0, 0))]

    out = lambda blk: pl.BlockSpec((rows, D_B), lambda b, j: (blk(b, j), 0))
    n_units = 2 * N_HEADS_B
    args = (qb, kbt, vb, bc, w0, grows, stat)
    return pl.pallas_call(
        functools.partial(_mlstm_kernel, cps=cps),
        grid=(batch, ns),
        in_specs=specs(fwd) + specs(bwd),
        out_specs=(out(fwd), out(bwd)),
        out_shape=(jax.ShapeDtypeStruct((n_tok, D_B), F32), jax.ShapeDtypeStruct((n_tok, D_B), F32)),
        scratch_shapes=[pltpu.VMEM((n_units, HEAD_DIM_B, 2 * HEAD_DIM_B), F32),
                        pltpu.VMEM((n_units, 1, LANES), F32)],
        compiler_params=_params(2),
        name="mlstm",
    )(*args, *args)


def _out_proj_kernel(x_ref, oa_ref, hf_ref, hb_ref, ob_ref, hg_ref, w_ref, o_ref):
    hsum = hf_ref[...] + hb_ref[...]
    parts = []
    for hd in range(N_HEADS_B):
        hh = hsum[:, hd * HEAD_DIM_B:(hd + 1) * HEAD_DIM_B]
        parts.append(hh * lax.rsqrt(jnp.mean(hh * hh, axis=-1, keepdims=True) + EPS))
    hn = jnp.concatenate(parts, axis=1) * hg_ref[...]
    out_b = (hn * _sigmoid(ob_ref[...])).astype(BF16)
    o_ref[...] = x_ref[...] + _dot(oa_ref[...], w_ref[:D_A]) + _dot(out_b, w_ref[D_A:])


def _out_proj(x2d, out_a, hf, hb, ob, hg, w, tm):
    n_tok = x2d.shape[0]
    tok = lambda d: pl.BlockSpec((tm, d), lambda i: (i, 0))
    return pl.pallas_call(
        _out_proj_kernel,
        grid=(n_tok // tm,),
        in_specs=[tok(D_MODEL), tok(D_A), tok(D_B), tok(D_B), tok(D_B),
                  pl.BlockSpec((1, D_B), lambda i: (0, 0)),
                  pl.BlockSpec((D_A + D_B, D_MODEL), lambda i: (0, 0))],
        out_specs=tok(D_MODEL),
        out_shape=jax.ShapeDtypeStruct((n_tok, D_MODEL), F32),
        compiler_params=_params(1),
        name="out_proj",
    )(x2d, out_a, hf, hb, ob, hg, w)


MXU_COLS = 256


def _ffn_kernel(xp_ref, x_ref, xn_ref, gain_ref, wu_ref, cw_ref, cb_ref, wd_ref, o_ref, *, tiles_per_seq):
    i = pl.program_id(0)
    tm = x_ref.shape[0]
    first = (i % tiles_per_seq) == 0
    last = (i % tiles_per_seq) == tiles_per_seq - 1
    hext = _normed_window(xp_ref, x_ref, xn_ref, gain_ref[...], first, last)
    acc = x_ref[...]
    for lo, hi in ((0, 5 * MXU_COLS), (5 * MXU_COLS, D_FF)):
        gcols = slice(lo, hi)
        vcols = slice(D_FF + lo, D_FF + hi)
        gate = _conv3(_dot(hext, wu_ref[:, gcols]), cw_ref[:, gcols], tm) + cb_ref[:, gcols]
        val = _conv3(_dot(hext, wu_ref[:, vcols]), cw_ref[:, vcols], tm) + cb_ref[:, vcols]
        acc = acc + _dot((gate * _sigmoid(gate) * val).astype(BF16), wd_ref[lo:hi, :])
    o_ref[...] = acc


def _ffn(x2d, seq, tm, gain, w_up, cw, cb, w_down):
    n_tok = x2d.shape[0]
    prev, main, nxt = _halo_specs(tm, D_MODEL, n_tok)
    full = lambda r: pl.BlockSpec((r, 2 * D_FF), lambda i: (0, 0))
    once = pl.Buffered(1)
    return pl.pallas_call(
        functools.partial(_ffn_kernel, tiles_per_seq=seq // tm),
        grid=(n_tok // tm,),
        in_specs=[prev, main, nxt, pl.BlockSpec((1, D_MODEL), lambda i: (0, 0)),
                  pl.BlockSpec((D_MODEL, 2 * D_FF), lambda i: (0, 0), pipeline_mode=once), full(3), full(1),
                  pl.BlockSpec((D_FF, D_MODEL), lambda i: (0, 0), pipeline_mode=once)],
        out_specs=pl.BlockSpec((tm, D_MODEL), lambda i: (i, 0)),
        out_shape=jax.ShapeDtypeStruct((n_tok, D_MODEL), F32),
        compiler_params=_params(1),
        name="ffn",
    )(x2d, x2d, x2d, gain, w_up, cw, cb, w_down)


def _conf_glu_kernel(x_ref, gain_ref, w_ref, b_ref, o_ref):
    h = _rms(x_ref[...], gain_ref[...]).astype(BF16)
    u = _dot(h, w_ref[...]) + b_ref[...]
    o_ref[...] = u[:, :D_MODEL] * _sigmoid(u[:, D_MODEL:])


def _conf_glu(x2d, gain, w, b, tm):
    n_tok = x2d.shape[0]
    return pl.pallas_call(
        _conf_glu_kernel,
        grid=(n_tok // tm,),
        in_specs=[pl.BlockSpec((tm, D_MODEL), lambda i: (i, 0)), pl.BlockSpec((1, D_MODEL), lambda i: (0, 0)),
                  pl.BlockSpec((D_MODEL, 2 * D_MODEL), lambda i: (0, 0)),
                  pl.BlockSpec((1, 2 * D_MODEL), lambda i: (0, 0))],
        out_specs=pl.BlockSpec((tm, D_MODEL), lambda i: (i, 0)),
        out_shape=jax.ShapeDtypeStruct((n_tok, D_MODEL), F32),
        compiler_params=_params(1),
        name="conf_glu",
    )(x2d, gain, w, b)


CONV_ROWS = 64


def _conf_conv_kernel(up_ref, u_ref, un_ref, x_ref, wdw_ref, bdw_ref, lng_ref, lnb_ref, w2_ref, b2_ref, o_ref,
                      rot_scr, act_scr, *, tiles_per_seq):
    i = pl.program_id(0)
    tm = u_ref.shape[0]
    n = tm + 2 * HALO
    first = (i % tiles_per_seq) == 0
    last = (i % tiles_per_seq) == tiles_per_seq - 1
    win = jnp.concatenate([jnp.where(first, 0.0, up_ref[...]), u_ref[...], jnp.where(last, 0.0, un_ref[...])], axis=0)
    rot_scr[0] = win
    for r in range(1, 8):
        rot_scr[r] = pltpu.roll(win, n - r, 0)

    for c in range(D_MODEL // LANES):
        lanes = slice(c * LANES, (c + 1) * LANES)
        taps = [wdw_ref[k, :, lanes] for k in range(CONV_C)]

        def block(rb, carry, lanes=lanes, taps=taps):
            r0 = pl.multiple_of(rb * CONV_ROWS, CONV_ROWS)
            for a in range(CONV_ROWS // SUBLANES):
                acc = None
                for k in range(CONV_C):
                    shift = k + 1
                    rows = pl.ds(r0 + (shift // SUBLANES + a) * SUBLANES, SUBLANES)
                    term = taps[k] * rot_scr[shift % SUBLANES, rows, lanes]
                    acc = term if acc is None else acc + term
                act_scr[pl.ds(r0 + a * SUBLANES, SUBLANES), lanes] = acc
            return carry

        lax.fori_loop(0, tm // CONV_ROWS, block, 0)
    conv = act_scr[...] + bdw_ref[...]
    xc = conv - jnp.mean(conv, axis=-1, keepdims=True)
    y = xc * lax.rsqrt(jnp.mean(xc * xc, axis=-1, keepdims=True) + EPS) * lng_ref[...] + lnb_ref[...]
    act = (y * _sigmoid(y)).astype(BF16)
    o_ref[...] = x_ref[...] + _dot(act, w2_ref[...]) + b2_ref[...]


def _conf_conv(u, x2d, seq, tm, wdw, bdw, lng, lnb, w2, b2):
    n_tok = x2d.shape[0]
    prev, main, nxt = _halo_specs(tm, D_MODEL, n_tok)
    vec = pl.BlockSpec((1, D_MODEL), lambda i: (0, 0))
    return pl.pallas_call(
        functools.partial(_conf_conv_kernel, tiles_per_seq=seq // tm),
        grid=(n_tok // tm,),
        in_specs=[prev, main, nxt, pl.BlockSpec((tm, D_MODEL), lambda i: (i, 0)),
                  pl.BlockSpec((CONV_C, SUBLANES, D_MODEL), lambda i: (0, 0, 0)), vec, vec, vec,
                  pl.BlockSpec((D_MODEL, D_MODEL), lambda i: (0, 0)), vec],
        out_specs=pl.BlockSpec((tm, D_MODEL), lambda i: (i, 0)),
        out_shape=jax.ShapeDtypeStruct((n_tok, D_MODEL), F32),
        scratch_shapes=[pltpu.VMEM((8, tm + 2 * HALO, D_MODEL), F32), pltpu.VMEM((tm, D_MODEL), F32)],
        compiler_params=_params(1),
        name="conf_conv",
    )(u, u, u, x2d, wdw, bdw, lng, lnb, w2, b2)


def _rope_tables(seq):
    pos = jnp.arange(seq)
    inv = ROPE_THETA ** (-jnp.arange(ROPE_PAIRS, dtype=F32) / ROPE_PAIRS)
    lane = np.arange(HEAD_DIM_A)
    section, half, pair = lane // (2 * ROPE_PAIRS), (lane // ROPE_PAIRS) % 2, lane % ROPE_PAIRS
    row_idx = (pos // GRID_W).astype(F32)[:, None]
    col_idx = (pos % GRID_W).astype(F32)[:, None]
    ang = jnp.where(jnp.asarray(section == 0)[None, :], row_idx, col_idx) * inv[pair][None, :]
    sign = jnp.asarray(np.where(half == 0, -1.0, 1.0), F32)[None, :]
    return jnp.tile(jnp.cos(ang), (1, 2)), jnp.tile(jnp.sin(ang) * sign, (1, 2))


def _segment_mean_matrix(n, width):
    seg = np.arange(n) // width
    return jnp.asarray((seg[:, None] == seg[None, :]).astype(np.float32) / width, BF16)


def _tile_size(seq, want):
    return min(want, seq)


def _trunk(x, p):
    batch, seq, _ = x.shape
    n_tok = batch * seq
    x2d = x.reshape(n_tok, D_MODEL)
    tm = _tile_size(seq, 512)
    cps = 4 if seq % (4 * CHUNK_B) == 0 else 2

    qt, k, vt, qb, kbt, vb, ob, g = _in_proj(x2d, seq, tm, p["mix_norm_e"], p["w_in"], p["cos"], p["sin"],
                                            p["segq"], p["segk"], p["qg"], p["kg"], p["w_qk_conv"], p["w_v_t"])
    out_a = _attention(qt, k, vt, batch, seq, _tile_size(seq, 512))
    bc, w0, grows, stat = _gate_prep(g, p["b_gates_row"], _tile_size(seq, 2 * CHUNK_B))
    hf, hb = _mlstm(qb, kbt, vb, bc, w0, grows, stat, batch, seq, cps)
    x2d = _out_proj(x2d, out_a, hf, hb, ob, p["h_gain"], p["w_out"], tm)
    x2d = _ffn(x2d, seq, tm, p["ffn_norm0"], p["w_up0"], p["w_dw_ff0"], p["b_dw_ff0"], p["w_down0"])
    u = _conf_glu(x2d, p["mix_norm_o"], p["w_pw1"], p["b_pw1"], tm)
    x2d = _conf_conv(u, x2d, seq, tm, p["w_dw_c"], p["b_dw_c"], p["ln_g"], p["ln_b"],
                     p["w_pw2"], p["b_pw2"])
    x2d = _ffn(x2d, seq, tm, p["ffn_norm1"], p["w_up1"], p["w_dw_ff1"], p["b_dw_ff1"], p["w_down1"])
    return x2d.reshape(batch, seq, D_MODEL)


def kernel(x_prompt, x_sample, mix_norm_e, w_in, q_gain_a, k_gain_a, w_qk_conv_b, b_gates_b, h_gain_b, w_out_e, mix_norm_o, w_pw1_c, b_pw1_c, w_dw_c, b_dw_c, ln_g_c, ln_b_c, w_pw2_c, b_pw2_c, ffn_norm, w_up, w_dw_ff, b_dw_ff, w_down):
    row = lambda a: a.reshape(1, -1).astype(F32)
    p = {
        "mix_norm_e": row(mix_norm_e[0]),
        "w_in": jnp.pad(w_in[0].astype(BF16), ((0, 0), (0, IN_COLS_PAD - IN_COLS))),
        "segq": _segment_mean_matrix(D_A, HEAD_DIM_A),
        "segk": _segment_mean_matrix(D_KV_A, HEAD_DIM_A),
        "qg": row(jnp.tile(q_gain_a[0], N_HEADS_A)),
        "kg": row(jnp.tile(k_gain_a[0], N_KV_HEADS_A)),
        "w_qk_conv": w_qk_conv_b[0].astype(F32),
        "w_v_t": w_in[0][:, _O_VA:_O_QKB].T.astype(BF16),
        "b_gates_row": jnp.pad(row(b_gates_b[0]), ((0, 0), (0, LANES - N_GATES_B))),
        "h_gain": row(h_gain_b[0]),
        "w_out": w_out_e[0].astype(BF16),
        "mix_norm_o": row(mix_norm_o[0]),
        "w_pw1": w_pw1_c[0].astype(BF16),
        "b_pw1": row(b_pw1_c[0]),
        "w_dw_c": jnp.broadcast_to(w_dw_c[0].astype(F32)[:, None, :], (CONV_C, SUBLANES, D_MODEL)),
        "b_dw_c": row(b_dw_c[0]),
        "ln_g": row(ln_g_c[0]),
        "ln_b": row(ln_b_c[0]),
        "w_pw2": w_pw2_c[0].astype(BF16),
        "b_pw2": row(b_pw2_c[0]),
    }
    for layer in range(2):
        p[f"ffn_norm{layer}"] = row(ffn_norm[layer])
        p[f"w_up{layer}"] = w_up[layer].astype(BF16)
        p[f"w_dw_ff{layer}"] = w_dw_ff[layer].astype(F32)
        p[f"b_dw_ff{layer}"] = row(b_dw_ff[layer])
        p[f"w_down{layer}"] = w_down[layer].astype(BF16)
    outs = []
    for x in (x_prompt, x_sample):
        p["cos"], p["sin"] = _rope_tables(x.shape[1])
        outs.append(_trunk(x, p))
    return tuple(outs)
```

```python
import functools

import numpy as np
import jax
import jax.numpy as jnp
from jax import lax
from jax.experimental import pallas as pl
from jax.experimental.pallas import tpu as pltpu

D_MODEL = 1024
GRID_W = 64
N_HEADS_A = 8
N_KV_HEADS_A = 2
HEAD_DIM_A = 64
D_A = N_HEADS_A * HEAD_DIM_A
D_KV_A = N_KV_HEADS_A * HEAD_DIM_A
ROPE_THETA = 10000.0
ROPE_PAIRS = HEAD_DIM_A // 4
N_HEADS_B = 4
HEAD_DIM_B = 128
D_B = N_HEADS_B * HEAD_DIM_B
CHUNK_B = 128
N_GATES_B = 4 * N_HEADS_B
IN_COLS = D_A + 2 * D_KV_A + 4 * D_B + N_GATES_B
IN_COLS_PAD = 2944
CONV_C = 31
D_FF = 2816
EPS = 1e-6
Q_SCALE = HEAD_DIM_A ** -0.5 * float(np.log2(np.e))

LANES = 128
SUBLANES = 8
HALO = 16
ONES_ROWS = 16
VMEM_LIMIT = 56 * 1024 * 1024

F32 = jnp.float32
BF16 = jnp.bfloat16

_O_QA, _O_KA, _O_VA, _O_QKB, _O_VB, _O_OB, _O_G = 0, 512, 640, 768, 1792, 2304, 2816


def _params(n_axes):
    return pltpu.CompilerParams(dimension_semantics=("arbitrary",) * n_axes,
                                vmem_limit_bytes=VMEM_LIMIT)


def _dot(a, b):
    return jnp.dot(a, b, preferred_element_type=F32)


def _rms(x, gain):
    ms = jnp.mean(x * x, axis=-1, keepdims=True)
    return x * lax.rsqrt(ms + EPS) * gain


def _sigmoid(x):
    return 1.0 / (1.0 + jnp.exp(-x))


def _normed_window(xp_ref, x_ref, xn_ref, gain, first, last):
    hp = jnp.where(first, 0.0, _rms(xp_ref[...], gain))
    hn = jnp.where(last, 0.0, _rms(xn_ref[...], gain))
    h = _rms(x_ref[...], gain)
    return jnp.concatenate([hp, h, hn], axis=0).astype(BF16)


def _conv3(u, cw, tm):
    n = u.shape[0]
    um = pltpu.roll(u, 1, 0)[HALO:HALO + tm]
    uc = u[HALO:HALO + tm]
    up = pltpu.roll(u, n - 1, 0)[HALO:HALO + tm]
    return um * cw[0:1] + uc * cw[1:2] + up * cw[2:3]


def _halo_specs(tm, d, n_tokens, axis=0, n_axes=1):
    r = tm // HALO
    last_blk = n_tokens // HALO - 1

    def pick(idx):
        return idx[axis]

    prev = pl.BlockSpec((HALO, d), lambda *idx: (jnp.maximum(pick(idx) * r - 1, 0), 0))
    main = pl.BlockSpec((tm, d), lambda *idx: (pick(idx), 0))
    nxt = pl.BlockSpec((HALO, d), lambda *idx: (jnp.minimum((pick(idx) + 1) * r, last_blk), 0))
    return prev, main, nxt


def _rope(xn, cos, sin, width):
    lane = lax.broadcasted_iota(jnp.int32, xn.shape, 1)
    first_half = (lane % (2 * ROPE_PAIRS)) < ROPE_PAIRS
    partner = jnp.where(first_half, pltpu.roll(xn, width - ROPE_PAIRS, 1), pltpu.roll(xn, ROPE_PAIRS, 1))
    return xn * cos + partner * sin


def _in_proj_kernel(xp_ref, x_ref, xn_ref, gain_ref, w_ref, cos_ref, sin_ref, segq_ref, segk_ref,
                    qg_ref, kg_ref, cw_ref, wvt_ref,
                    qt_out, k_out, vt_out, qb_out, kbt_out, vb_out, ob_out, g_out, *, tiles_per_seq):
    i = pl.program_id(0)
    tm = x_ref.shape[0]
    first = (i % tiles_per_seq) == 0
    last = (i % tiles_per_seq) == tiles_per_seq - 1
    hext = _normed_window(xp_ref, x_ref, xn_ref, gain_ref[...], first, last)
    h = hext[HALO:HALO + tm]

    cos2 = cos_ref[...]
    sin2 = sin_ref[...]
    qa = _dot(h, w_ref[:, _O_QA:_O_KA])
    ka = _dot(h, w_ref[:, _O_KA:_O_VA])
    ms = _dot((qa * qa).astype(BF16), segq_ref[...])
    msk = _dot((ka * ka).astype(BF16), segk_ref[...])
    u = _dot(hext, w_ref[:, _O_QKB:_O_VB])

    qn = qa * lax.rsqrt(ms + EPS) * qg_ref[...]
    cos = jnp.concatenate([cos2] * (D_A // LANES), axis=1)
    sin = jnp.concatenate([sin2] * (D_A // LANES), axis=1)
    qt_out[...] = (_rope(qn, cos, sin, D_A) * Q_SCALE).T.astype(BF16)
    kn = ka * lax.rsqrt(msk + EPS) * kg_ref[...]
    k_out[...] = _rope(kn, cos2, sin2, D_KV_A).astype(BF16)

    c = _conv3(u, cw_ref[...], tm)
    act = c * _sigmoid(c)
    qb_out[...] = act[:, :D_B].astype(BF16)
    kbt_out[...] = (act[:, D_B:] * (HEAD_DIM_B ** -0.5)).T.astype(BF16)

    vb_out[...] = _dot(h, w_ref[:, _O_VB:_O_OB]).astype(BF16)
    ob_out[...] = _dot(h, w_ref[:, _O_OB:_O_G])
    g_out[...] = _dot(h, w_ref[:, _O_G:IN_COLS_PAD])
    vt = lax.dot_general(wvt_ref[...], h, (((1,), (1,)), ((), ())), preferred_element_type=F32)
    vt_out[0] = vt.astype(BF16)


def _in_proj(x2d, seq, tm, gain, w_pad, cos, sin, segq, segk, qg, kg, cw, wvt):
    n_tok = x2d.shape[0]
    nt = n_tok // tm
    tps = seq // tm
    prev, main, nxt = _halo_specs(tm, D_MODEL, n_tok)
    const = lambda shape: pl.BlockSpec(shape, lambda i: (0,) * len(shape))
    rope_spec = pl.BlockSpec((tm, LANES), lambda i: (i % tps, 0))
    tok = lambda d: pl.BlockSpec((tm, d), lambda i: (i, 0))
    out_shape = (
        jax.ShapeDtypeStruct((D_A, n_tok), BF16),
        jax.ShapeDtypeStruct((n_tok, D_KV_A), BF16),
        jax.ShapeDtypeStruct((nt, D_KV_A, tm), BF16),
        jax.ShapeDtypeStruct((n_tok, D_B), BF16),
        jax.ShapeDtypeStruct((D_B, n_tok), BF16),
        jax.ShapeDtypeStruct((n_tok, D_B), BF16),
        jax.ShapeDtypeStruct((n_tok, D_B), F32),
        jax.ShapeDtypeStruct((n_tok, LANES), F32),
    )
    out_specs = (pl.BlockSpec((D_A, tm), lambda i: (0, i)), tok(D_KV_A),
                 pl.BlockSpec((1, D_KV_A, tm), lambda i: (i, 0, 0)),
                 tok(D_B), pl.BlockSpec((D_B, tm), lambda i: (0, i)), tok(D_B), tok(D_B), tok(LANES))
    return pl.pallas_call(
        functools.partial(_in_proj_kernel, tiles_per_seq=tps),
        grid=(nt,),
        in_specs=[prev, main, nxt, const((1, D_MODEL)), const((D_MODEL, IN_COLS_PAD)), rope_spec, rope_spec,
                  const((D_A, D_A)), const((D_KV_A, D_KV_A)), const((1, D_A)), const((1, D_KV_A)),
                  const((3, 2 * D_B)), const((D_KV_A, D_MODEL))],
        out_specs=out_specs,
        out_shape=out_shape,
        compiler_params=_params(1),
        name="in_proj",
    )(x2d, x2d, x2d, gain, w_pad, cos, sin, segq, segk, qg, kg, cw, wvt)


def _attn_kernel(qt_ref, k_ref, vt_ref, o_ref, s_a, s_b, mx_a, mx_b, *, n_chunks, kc):
    tq = qt_ref.shape[1]
    dh = HEAD_DIM_A
    group = N_HEADS_A // N_KV_HEADS_A
    zeros = jnp.zeros((dh, tq), BF16)
    ones = jnp.ones((ONES_ROWS, kc), BF16)
    rhs = []
    for h in range(N_HEADS_A):
        qh = qt_ref[h * dh:(h + 1) * dh, :]
        rhs.append(jnp.concatenate([qh, zeros] if h // group == 0 else [zeros, qh], axis=0))

    def score1(c, j, s_scr, mx_scr):
        kblk = k_ref[pl.ds(pl.multiple_of(c * kc, kc), kc), :]
        s = _dot(kblk, rhs[j])
        s_scr[j] = s
        mx_scr[j] = jnp.max(s, axis=0, keepdims=True)

    def update1(c, j, s_scr, mx_scr, state):
        g = j // group
        vblk = jnp.concatenate([vt_ref[c, g * dh:(g + 1) * dh, :], ones], axis=0)
        m, acc = state
        m_new = jnp.maximum(m, mx_scr[j])
        p = jnp.exp2(s_scr[j] - m_new)
        acc = jnp.exp2(m - m_new) * acc + _dot(vblk, p.astype(BF16))
        return m_new, acc

    def scores(c, s_scr, mx_scr):
        for j in range(N_HEADS_A):
            score1(c, j, s_scr, mx_scr)

    def update(c, s_scr, mx_scr, carry):
        return tuple(update1(c, j, s_scr, mx_scr, state) for j, state in enumerate(carry))

    def fused(c_next, s_next, mx_next, c, s_cur, mx_cur, carry):
        out = []
        for j, state in enumerate(carry):
            score1(c_next, j, s_next, mx_next)
            out.append(update1(c, j, s_cur, mx_cur, state))
        return tuple(out)

    def body(i, carry):
        c = 2 * i
        carry = fused(c + 1, s_b, mx_b, c, s_a, mx_a, carry)
        return fused(c + 2, s_a, mx_a, c + 1, s_b, mx_b, carry)

    init = (jnp.full((1, tq), -jnp.inf, F32), jnp.zeros((dh + ONES_ROWS, tq), F32))
    scores(0, s_a, mx_a)
    carry = lax.fori_loop(0, n_chunks // 2 - 1, body, (init,) * N_HEADS_A)
    carry = fused(n_chunks - 1, s_b, mx_b, n_chunks - 2, s_a, mx_a, carry)
    carry = update(n_chunks - 1, s_b, mx_b, carry)
    for pair in range(N_HEADS_A // 2):
        ot = jnp.concatenate([acc[:dh] / acc[dh:dh + 1] for (_, acc) in carry[2 * pair:2 * pair + 2]],
                             axis=0)
        o_ref[:, pair * LANES:(pair + 1) * LANES] = ot.T.astype(BF16)


def _attention(qt, k, vt, batch, seq, tq):
    n_tok = k.shape[0]
    kc = vt.shape[2]
    nq = seq // tq
    n_chunks = seq // kc
    assert n_chunks % 2 == 0, "the key loop handles chunks in pairs"
    return pl.pallas_call(
        functools.partial(_attn_kernel, n_chunks=n_chunks, kc=kc),
        grid=(batch, nq),
        in_specs=[pl.BlockSpec((D_A, tq), lambda b, i: (0, b * nq + i)),
                  pl.BlockSpec((seq, D_KV_A), lambda b, i: (b, 0)),
                  pl.BlockSpec((n_chunks, D_KV_A, kc), lambda b, i: (b, 0, 0))],
        out_specs=pl.BlockSpec((tq, D_A), lambda b, i: (b * nq + i, 0)),
        out_shape=jax.ShapeDtypeStruct((n_tok, D_A), BF16),
        scratch_shapes=[pltpu.VMEM((N_HEADS_A, kc, tq), F32), pltpu.VMEM((N_HEADS_A, kc, tq), F32),
                        pltpu.VMEM((N_HEADS_A, 1, tq), F32), pltpu.VMEM((N_HEADS_A, 1, tq), F32)],
        compiler_params=_params(2),
        name="attention",
    )(qt, k, vt)


def _log_sigmoid(x):
    return jnp.minimum(x, 0.0) - jnp.log1p(jnp.exp(-jnp.abs(x)))


def _split3(x):
    hi = x.astype(BF16)
    r1 = x - hi.astype(F32)
    mid = r1.astype(BF16)
    lo = (r1 - mid.astype(F32)).astype(BF16)
    return hi, mid, lo


GATE_ROWS = 2 * N_GATES_B


def _gate_prep_kernel(g_ref, bias_ref, bc_out, w0_out, rows_out, stat_out):
    tg = g_ref.shape[0]
    L = CHUNK_B
    n_chunk = tg // L
    gcol = g_ref[...] + bias_ref[...]
    pieces = _split3(_log_sigmoid(gcol))
    r = lax.broadcasted_iota(jnp.int32, (tg, tg), 0)
    c = lax.broadcasted_iota(jnp.int32, (tg, tg), 1)
    log2_l = L.bit_length() - 1
    same = lax.shift_right_logical(r, log2_l) == lax.shift_right_logical(c, log2_l)

    def chunk_sums(msk):
        sel = jnp.where(msk, 1.0, 0.0).astype(BF16)
        return sum(_dot(sel, piece) for piece in pieces)

    pre = chunk_sums(jnp.logical_and(same, c <= r))
    suf = chunk_sums(jnp.logical_and(same, c >= r))
    tot = chunk_sums(same)
    fwd_lane = lax.broadcasted_iota(jnp.int32, (tg, LANES), 1) < 2 * N_HEADS_B
    bc = jnp.where(fwd_lane, pre, suf)
    to_input_lanes = lambda x: pltpu.roll(x, LANES - N_HEADS_B, 1)
    tot_i = to_input_lanes(tot)
    w_log = (tot_i - to_input_lanes(bc) + gcol).reshape(n_chunk, L, LANES)
    wmax = jnp.max(w_log, axis=1, keepdims=True)
    bc_out[...] = bc
    w0_out[...] = jnp.exp(w_log - wmax).reshape(tg, LANES)
    rows_out[0:N_GATES_B, :] = gcol.T[0:N_GATES_B]
    rows_out[N_GATES_B:GATE_ROWS, :] = bc.T[0:N_GATES_B]
    stat_out[:, 0:1, :] = tot_i.reshape(n_chunk, L, LANES)[:, 0:1, :]
    stat_out[:, 1:2, :] = wmax
    stat_out[:, 2:SUBLANES, :] = jnp.zeros((n_chunk, SUBLANES - 2, LANES), F32)


def _gate_prep(g, bias_row, tg):
    n_tok = g.shape[0]
    per = tg // CHUNK_B
    tok = pl.BlockSpec((tg, LANES), lambda i: (i, 0))
    return pl.pallas_call(
        _gate_prep_kernel,
        grid=(n_tok // tg,),
        in_specs=[tok, pl.BlockSpec((1, LANES), lambda i: (0, 0))],
        out_specs=(tok, tok, pl.BlockSpec((GATE_ROWS, tg), lambda i: (0, i)),
                   pl.BlockSpec((per, SUBLANES, LANES), lambda i: (i, 0, 0))),
        out_shape=(jax.ShapeDtypeStruct((n_tok, LANES), F32), jax.ShapeDtypeStruct((n_tok, LANES), F32),
                   jax.ShapeDtypeStruct((GATE_ROWS, n_tok), F32),
                   jax.ShapeDtypeStruct((n_tok // CHUNK_B, SUBLANES, LANES), F32)),
        compiler_params=_params(1),
        name="gate_prep",
    )(g, bias_row)


def _mlstm_kernel(qf_ref, ktf_ref, vf_ref, bcf_ref, w0f_ref, grf_ref, stf_ref,
                  qb_ref, ktb_ref, vb_ref, bcb_ref, w0b_ref, grb_ref, stb_ref,
                  hf_out, hb_out, ct_scr, m_scr, *, cps):
    j = pl.program_id(1)
    L = CHUNK_B
    dh = HEAD_DIM_B

    @pl.when(j == 0)
    def _():
        ct_scr[...] = jnp.zeros_like(ct_scr)
        m_scr[...] = jnp.zeros_like(m_scr)

    row = lax.broadcasted_iota(jnp.int32, (L, L), 0)
    col = lax.broadcasted_iota(jnp.int32, (L, L), 1)
    lower = col <= row
    upper = col >= row
    ones = jnp.ones((L, dh), BF16)
    sel_row = lax.broadcasted_iota(jnp.int32, (LANES, LANES), 0)

    dirs = ((qf_ref, ktf_ref, vf_ref, bcf_ref, w0f_ref, grf_ref, stf_ref, hf_out),
            (qb_ref, ktb_ref, vb_ref, bcb_ref, w0b_ref, grb_ref, stb_ref, hb_out))
    units = []
    for cc in range(cps):
        for d, (q_ref, kt_ref, v_ref, bc_ref, w0_ref, gr_ref, st_ref, out_ref) in enumerate(dirs):
            ci = cc if d == 0 else cps - 1 - cc
            rows = slice(ci * L, (ci + 1) * L)
            bc_pieces = _split3(bc_ref[rows, :])
            w0_all = w0_ref[rows, :].astype(BF16)
            stat_pieces = _split3(st_ref[ci])
            for hd in range(N_HEADS_B):
                u = dict(unit=d * N_HEADS_B + hd, mask=lower if d == 0 else upper, out_ref=out_ref, rows=rows,
                         lanes=slice(hd * dh, (hd + 1) * dh))
                ci_col = d * 2 * N_HEADS_B + hd
                cf_col = ci_col + N_HEADS_B
                pick_f = jnp.where(sel_row == cf_col, 1.0, 0.0).astype(BF16)
                pick_i = jnp.where(sel_row == ci_col, 1.0, 0.0).astype(BF16)
                u["b_c"] = sum(_dot(piece, pick_f) for piece in bc_pieces)
                stat = sum(_dot(piece, pick_i) for piece in stat_pieces)
                u["b_last"], u["wmax"] = stat[0:1], stat[1:2]
                u["w0"] = _dot(w0_all, pick_i)
                u["i_r"] = gr_ref[ci_col:ci_col + 1, rows]
                u["b_r"] = gr_ref[N_GATES_B + cf_col:N_GATES_B + cf_col + 1, rows]
                u["q"] = q_ref[rows, u["lanes"]]
                u["kt"] = kt_ref[u["lanes"], rows]
                u["v_aug"] = jnp.concatenate([v_ref[rows, u["lanes"]], ones], axis=1)
                u["qk"] = _dot(u["q"], u["kt"])
                units.append(u)

    for u in units:
        log_d = jnp.where(u["mask"], u["b_c"] - u["b_r"] + u["i_r"], -jnp.inf)
        u["a"] = jnp.max(log_d, axis=-1, keepdims=True)
        u["p"] = (u["qk"] * jnp.exp(log_d - u["a"])).astype(BF16)
        u["wv"] = (jnp.concatenate([u["w0"], u["w0"]], axis=1) * u["v_aug"].astype(F32)).astype(BF16)
    for u in units:
        u["intra"] = _dot(u["p"], u["v_aug"])
        u["upd"] = _dot(u["kt"], u["wv"])

    for u in units:
        m = m_scr[u["unit"]]
        ct = ct_scr[u["unit"]]
        inter = u["b_c"] + m
        m_row = jnp.maximum(u["a"], inter)
        w_intra = jnp.exp(u["a"] - m_row)
        w_inter = jnp.exp(inter - m_row)
        cross = _dot(u["q"], ct.astype(BF16))
        den = w_inter * cross[:, dh:] + w_intra * u["intra"][:, dh:]
        inv = 1.0 / jnp.maximum(jnp.abs(den), jnp.exp(-m_row))
        u["out_ref"][u["rows"], u["lanes"]] = ((w_inter * inv) * cross[:, :dh]
                                               + (w_intra * inv) * u["intra"][:, :dh])
        m_new = jnp.maximum(u["b_last"] + m, u["wmax"])
        decay = jnp.exp(u["b_last"] + m - m_new)
        gain = jnp.exp(u["wmax"] - m_new)
        ct_scr[u["unit"]] = (jnp.concatenate([decay, decay], axis=1) * ct
                             + jnp.concatenate([gain, gain], axis=1) * u["upd"])
        m_scr[u["unit"]] = m_new


def _mlstm(qb, kbt, vb, bc, w0, grows, stat, batch, seq, cps):
    n_tok = qb.shape[0]
    rows = cps * CHUNK_B
    ns = seq // rows
    fwd = lambda b, j: b * ns + j
    bwd = lambda b, j: b * ns + ns - 1 - j

    def specs(blk):
        return [pl.BlockSpec((rows, D_B), lambda b, j: (blk(b, j), 0)),
                pl.BlockSpec((D_B, rows), lambda b, j: (0, blk(b, j))),
                pl.BlockSpec((rows, D_B), lambda b, j: (blk(b, j), 0)),
                pl.BlockSpec((rows, LANES), lambda b, j: (blk(b, j), 0)),
                pl.BlockSpec((rows, LANES), lambda b, j: (blk(b, j), 0)),
                pl.BlockSpec((GATE_ROWS, rows), lambda b, j: (0, blk(b, j))),
                pl.BlockSpec((cps, SUBLANES, LANES), lambda b, j: (blk(b, j), 0, 0))]

    out = lambda blk: pl.BlockSpec((rows, D_B), lambda b, j: (blk(b, j), 0))
    n_units = 2 * N_HEADS_B
    args = (qb, kbt, vb, bc, w0, grows, stat)
    return pl.pallas_call(
        functools.partial(_mlstm_kernel, cps=cps),
        grid=(batch, ns),
        in_specs=specs(fwd) + specs(bwd),
        out_specs=(out(fwd), out(bwd)),
        out_shape=(jax.ShapeDtypeStruct((n_tok, D_B), F32), jax.ShapeDtypeStruct((n_tok, D_B), F32)),
        scratch_shapes=[pltpu.VMEM((n_units, HEAD_DIM_B, 2 * HEAD_DIM_B), F32),
                        pltpu.VMEM((n_units, 1, LANES), F32)],
        compiler_params=_params(2),
        name="mlstm",
    )(*args, *args)


def _out_proj_kernel(x_ref, oa_ref, hf_ref, hb_ref, ob_ref, hg_ref, w_ref, o_ref):
    hsum = hf_ref[...] + hb_ref[...]
    parts = []
    for hd in range(N_HEADS_B):
        hh = hsum[:, hd * HEAD_DIM_B:(hd + 1) * HEAD_DIM_B]
        parts.append(hh * lax.rsqrt(jnp.mean(hh * hh, axis=-1, keepdims=True) + EPS))
    hn = jnp.concatenate(parts, axis=1) * hg_ref[...]
    out_b = (hn * _sigmoid(ob_ref[...])).astype(BF16)
    o_ref[...] = x_ref[...] + _dot(oa_ref[...], w_ref[:D_A]) + _dot(out_b, w_ref[D_A:])


def _out_proj(x2d, out_a, hf, hb, ob, hg, w, tm):
    n_tok = x2d.shape[0]
    tok = lambda d: pl.BlockSpec((tm, d), lambda i: (i, 0))
    return pl.pallas_call(
        _out_proj_kernel,
        grid=(n_tok // tm,),
        in_specs=[tok(D_MODEL), tok(D_A), tok(D_B), tok(D_B), tok(D_B),
                  pl.BlockSpec((1, D_B), lambda i: (0, 0)),
                  pl.BlockSpec((D_A + D_B, D_MODEL), lambda i: (0, 0))],
        out_specs=tok(D_MODEL),
        out_shape=jax.ShapeDtypeStruct((n_tok, D_MODEL), F32),
        compiler_params=_params(1),
        name="out_proj",
    )(x2d, out_a, hf, hb, ob, hg, w)


MXU_COLS = 256


def _ffn_kernel(xp_ref, x_ref, xn_ref, gain_ref, wu_ref, cw_ref, cb_ref, wd_ref, o_ref, *, tiles_per_seq):
    i = pl.program_id(0)
    tm = x_ref.shape[0]
    first = (i % tiles_per_seq) == 0
    last = (i % tiles_per_seq) == tiles_per_seq - 1
    hext = _normed_window(xp_ref, x_ref, xn_ref, gain_ref[...], first, last)
    acc = x_ref[...]
    for lo, hi in ((0, 5 * MXU_COLS), (5 * MXU_COLS, D_FF)):
        gcols = slice(lo, hi)
        vcols = slice(D_FF + lo, D_FF + hi)
        gate = _conv3(_dot(hext, wu_ref[:, gcols]), cw_ref[:, gcols], tm) + cb_ref[:, gcols]
        val = _conv3(_dot(hext, wu_ref[:, vcols]), cw_ref[:, vcols], tm) + cb_ref[:, vcols]
        acc = acc + _dot((gate * _sigmoid(gate) * val).astype(BF16), wd_ref[lo:hi, :])
    o_ref[...] = acc


def _ffn(x2d, seq, tm, gain, w_up, cw, cb, w_down):
    n_tok = x2d.shape[0]
    prev, main, nxt = _halo_specs(tm, D_MODEL, n_tok)
    full = lambda r: pl.BlockSpec((r, 2 * D_FF), lambda i: (0, 0))
    once = pl.Buffered(1)
    return pl.pallas_call(
        functools.partial(_ffn_kernel, tiles_per_seq=seq // tm),
        grid=(n_tok // tm,),
        in_specs=[prev, main, nxt, pl.BlockSpec((1, D_MODEL), lambda i: (0, 0)),
                  pl.BlockSpec((D_MODEL, 2 * D_FF), lambda i: (0, 0), pipeline_mode=once), full(3), full(1),
                  pl.BlockSpec((D_FF, D_MODEL), lambda i: (0, 0), pipeline_mode=once)],
        out_specs=pl.BlockSpec((tm, D_MODEL), lambda i: (i, 0)),
        out_shape=jax.ShapeDtypeStruct((n_tok, D_MODEL), F32),
        compiler_params=_params(1),
        name="ffn",
    )(x2d, x2d, x2d, gain, w_up, cw, cb, w_down)


def _conf_glu_kernel(x_ref, gain_ref, w_ref, b_ref, o_ref):
    h = _rms(x_ref[...], gain_ref[...]).astype(BF16)
    u = _dot(h, w_ref[...]) + b_ref[...]
    o_ref[...] = u[:, :D_MODEL] * _sigmoid(u[:, D_MODEL:])


def _conf_glu(x2d, gain, w, b, tm):
    n_tok = x2d.shape[0]
    return pl.pallas_call(
        _conf_glu_kernel,
        grid=(n_tok // tm,),
        in_specs=[pl.BlockSpec((tm, D_MODEL), lambda i: (i, 0)), pl.BlockSpec((1, D_MODEL), lambda i: (0, 0)),
                  pl.BlockSpec((D_MODEL, 2 * D_MODEL), lambda i: (0, 0)),
                  pl.BlockSpec((1, 2 * D_MODEL), lambda i: (0, 0))],
        out_specs=pl.BlockSpec((tm, D_MODEL), lambda i: (i, 0)),
        out_shape=jax.ShapeDtypeStruct((n_tok, D_MODEL), F32),
        compiler_params=_params(1),
        name="conf_glu",
    )(x2d, gain, w, b)


CONV_ROWS = 64


def _conf_conv_kernel(up_ref, u_ref, un_ref, x_ref, wdw_ref, bdw_ref, lng_ref, lnb_ref, w2_ref, b2_ref, o_ref,
                      rot_scr, act_scr, *, tiles_per_seq):
    i = pl.program_id(0)
    tm = u_ref.shape[0]
    n = tm + 2 * HALO
    first = (i % tiles_per_seq) == 0
    last = (i % tiles_per_seq) == tiles_per_seq - 1
    win = jnp.concatenate([jnp.where(first, 0.0, up_ref[...]), u_ref[...], jnp.where(last, 0.0, un_ref[...])], axis=0)
    rot_scr[0] = win
    for r in range(1, 8):
        rot_scr[r] = pltpu.roll(win, n - r, 0)

    for c in range(D_MODEL // LANES):
        lanes = slice(c * LANES, (c + 1) * LANES)
        taps = [wdw_ref[k, :, lanes] for k in range(CONV_C)]

        def block(rb, carry, lanes=lanes, taps=taps):
            r0 = pl.multiple_of(rb * CONV_ROWS, CONV_ROWS)
            for a in range(CONV_ROWS // SUBLANES):
                acc = None
                for k in range(CONV_C):
                    shift = k + 1
                    rows = pl.ds(r0 + (shift // SUBLANES + a) * SUBLANES, SUBLANES)
                    term = taps[k] * rot_scr[shift % SUBLANES, rows, lanes]
                    acc = term if acc is None else acc + term
                act_scr[pl.ds(r0 + a * SUBLANES, SUBLANES), lanes] = acc
            return carry

        lax.fori_loop(0, tm // CONV_ROWS, block, 0)
    conv = act_scr[...] + bdw_ref[...]
    xc = conv - jnp.mean(conv, axis=-1, keepdims=True)
    y = xc * lax.rsqrt(jnp.mean(xc * xc, axis=-1, keepdims=True) + EPS) * lng_ref[...] + lnb_ref[...]
    act = (y * _sigmoid(y)).astype(BF16)
    o_ref[...] = x_ref[...] + _dot(act, w2_ref[...]) + b2_ref[...]


def _conf_conv(u, x2d, seq, tm, wdw, bdw, lng, lnb, w2, b2):
    n_tok = x2d.shape[0]
    prev, main, nxt = _halo_specs(tm, D_MODEL, n_tok)
    vec = pl.BlockSpec((1, D_MODEL), lambda i: (0, 0))
    return pl.pallas_call(
        functools.partial(_conf_conv_kernel, tiles_per_seq=seq // tm),
        grid=(n_tok // tm,),
        in_specs=[prev, main, nxt, pl.BlockSpec((tm, D_MODEL), lambda i: (i, 0)),
                  pl.BlockSpec((CONV_C, SUBLANES, D_MODEL), lambda i: (0, 0, 0)), vec, vec, vec,
                  pl.BlockSpec((D_MODEL, D_MODEL), lambda i: (0, 0)), vec],
        out_specs=pl.BlockSpec((tm, D_MODEL), lambda i: (i, 0)),
        out_shape=jax.ShapeDtypeStruct((n_tok, D_MODEL), F32),
        scratch_shapes=[pltpu.VMEM((8, tm + 2 * HALO, D_MODEL), F32), pltpu.VMEM((tm, D_MODEL), F32)],
        compiler_params=_params(1),
        name="conf_conv",
    )(u, u, u, x2d, wdw, bdw, lng, lnb, w2, b2)


def _rope_tables(seq):
    pos = jnp.arange(seq)
    inv = ROPE_THETA ** (-jnp.arange(ROPE_PAIRS, dtype=F32) / ROPE_PAIRS)
    lane = np.arange(HEAD_DIM_A)
    section, half, pair = lane // (2 * ROPE_PAIRS), (lane // ROPE_PAIRS) % 2, lane % ROPE_PAIRS
    row_idx = (pos // GRID_W).astype(F32)[:, None]
    col_idx = (pos % GRID_W).astype(F32)[:, None]
    ang = jnp.where(jnp.asarray(section == 0)[None, :], row_idx, col_idx) * inv[pair][None, :]
    sign = jnp.asarray(np.where(half == 0, -1.0, 1.0), F32)[None, :]
    return jnp.tile(jnp.cos(ang), (1, 2)), jnp.tile(jnp.sin(ang) * sign, (1, 2))


def _segment_mean_matrix(n, width):
    seg = np.arange(n) // width
    return jnp.asarray((seg[:, None] == seg[None, :]).astype(np.float32) / width, BF16)


def _tile_size(seq, want):
    return min(want, seq)


def _trunk(x, p):
    batch, seq, _ = x.shape
    n_tok = batch * seq
    x2d = x.reshape(n_tok, D_MODEL)
    tm = _tile_size(seq, 512)
    cps = 4 if seq % (4 * CHUNK_B) == 0 else 2

    qt, k, vt, qb, kbt, vb, ob, g = _in_proj(x2d, seq, tm, p["mix_norm_e"], p["w_in"], p["cos"], p["sin"],
                                            p["segq"], p["segk"], p["qg"], p["kg"], p["w_qk_conv"], p["w_v_t"])
    out_a = _attention(qt, k, vt, batch, seq, _tile_size(seq, 512))
    bc, w0, grows, stat = _gate_prep(g, p["b_gates_row"], _tile_size(seq, 2 * CHUNK_B))
    hf, hb = _mlstm(qb, kbt, vb, bc, w0, grows, stat, batch, seq, cps)
    x2d = _out_proj(x2d, out_a, hf, hb, ob, p["h_gain"], p["w_out"], tm)
    x2d = _ffn(x2d, seq, tm, p["ffn_norm0"], p["w_up0"], p["w_dw_ff0"], p["b_dw_ff0"], p["w_down0"])
    u = _conf_glu(x2d, p["mix_norm_o"], p["w_pw1"], p["b_pw1"], tm)
    x2d = _conf_conv(u, x2d, seq, tm, p["w_dw_c"], p["b_dw_c"], p["ln_g"], p["ln_b"],
                     p["w_pw2"], p["b_pw2"])
    x2d = _ffn(x2d, seq, tm, p["ffn_norm1"], p["w_up1"], p["w_dw_ff1"], p["b_dw_ff1"], p["w_down1"])
    return x2d.reshape(batch, seq, D_MODEL)


def kernel(x_prompt, x_sample, mix_norm_e, w_in, q_gain_a, k_gain_a, w_qk_conv_b, b_gates_b, h_gain_b, w_out_e, mix_norm_o, w_pw1_c, b_pw1_c, w_dw_c, b_dw_c, ln_g_c, ln_b_c, w_pw2_c, b_pw2_c, ffn_norm, w_up, w_dw_ff, b_dw_ff, w_down):
    row = lambda a: a.reshape(1, -1).astype(F32)
    p = {
        "mix_norm_e": row(mix_norm_e[0]),
        "w_in": jnp.pad(w_in[0].astype(BF16), ((0, 0), (0, IN_COLS_PAD - IN_COLS))),
        "segq": _segment_mean_matrix(D_A, HEAD_DIM_A),
        "segk": _segment_mean_matrix(D_KV_A, HEAD_DIM_A),
        "qg": row(jnp.tile(q_gain_a[0], N_HEADS_A)),
        "kg": row(jnp.tile(k_gain_a[0], N_KV_HEADS_A)),
        "w_qk_conv": w_qk_conv_b[0].astype(F32),
        "w_v_t": w_in[0][:, _O_VA:_O_QKB].T.astype(BF16),
        "b_gates_row": jnp.pad(row(b_gates_b[0]), ((0, 0), (0, LANES - N_GATES_B))),
        "h_gain": row(h_gain_b[0]),
        "w_out": w_out_e[0].astype(BF16),
        "mix_norm_o": row(mix_norm_o[0]),
        "w_pw1": w_pw1_c[0].astype(BF16),
        "b_pw1": row(b_pw1_c[0]),
        "w_dw_c": jnp.broadcast_to(w_dw_c[0].astype(F32)[:, None, :], (CONV_C, SUBLANES, D_MODEL)),
        "b_dw_c": row(b_dw_c[0]),
        "ln_g": row(ln_g_c[0]),
        "ln_b": row(ln_b_c[0]),
        "w_pw2": w_pw2_c[0].astype(BF16),
        "b_pw2": row(b_pw2_c[0]),
    }
    for layer in range(2):
        p[f"ffn_norm{layer}"] = row(ffn_norm[layer])
        p[f"w_up{layer}"] = w_up[layer].astype(BF16)
        p[f"w_dw_ff{layer}"] = w_dw_ff[layer].astype(F32)
        p[f"b_dw_ff{layer}"] = row(b_dw_ff[layer])
        p[f"w_down{layer}"] = w_down[layer].astype(BF16)
    outs = []
    for x in (x_prompt, x_sample):
        p["cos"], p["sin"] = _rope_tables(x.shape[1])
        outs.append(_trunk(x, p))
    return tuple(outs)
```

```python
import functools

import numpy as np
import jax
import jax.numpy as jnp
from jax import lax
from jax.experimental import pallas as pl
from jax.experimental.pallas import tpu as pltpu

D_MODEL = 1024
GRID_W = 64
N_HEADS_A = 8
N_KV_HEADS_A = 2
HEAD_DIM_A = 64
D_A = N_HEADS_A * HEAD_DIM_A
D_KV_A = N_KV_HEADS_A * HEAD_DIM_A
ROPE_THETA = 10000.0
ROPE_PAIRS = HEAD_DIM_A // 4
N_HEADS_B = 4
HEAD_DIM_B = 128
D_B = N_HEADS_B * HEAD_DIM_B
CHUNK_B = 128
N_GATES_B = 4 * N_HEADS_B
IN_COLS = D_A + 2 * D_KV_A + 4 * D_B + N_GATES_B
IN_COLS_PAD = 2944
CONV_C = 31
D_FF = 2816
EPS = 1e-6
Q_SCALE = HEAD_DIM_A ** -0.5 * float(np.log2(np.e))

LANES = 128
SUBLANES = 8
HALO = 16
ONES_ROWS = 16
VMEM_LIMIT = 56 * 1024 * 1024

F32 = jnp.float32
BF16 = jnp.bfloat16

_O_QA, _O_KA, _O_VA, _O_QKB, _O_VB, _O_OB, _O_G = 0, 512, 640, 768, 1792, 2304, 2816


def _params(n_axes):
    return pltpu.CompilerParams(dimension_semantics=("arbitrary",) * n_axes,
                                vmem_limit_bytes=VMEM_LIMIT)


def _dot(a, b):
    return jnp.dot(a, b, preferred_element_type=F32)


def _rms(x, gain):
    ms = jnp.mean(x * x, axis=-1, keepdims=True)
    return x * lax.rsqrt(ms + EPS) * gain


def _sigmoid(x):
    return 1.0 / (1.0 + jnp.exp(-x))


def _normed_window(xp_ref, x_ref, xn_ref, gain, first, last):
    hp = jnp.where(first, 0.0, _rms(xp_ref[...], gain))
    hn = jnp.where(last, 0.0, _rms(xn_ref[...], gain))
    h = _rms(x_ref[...], gain)
    return jnp.concatenate([hp, h, hn], axis=0).astype(BF16)


def _conv3(u, cw, tm):
    n = u.shape[0]
    um = pltpu.roll(u, 1, 0)[HALO:HALO + tm]
    uc = u[HALO:HALO + tm]
    up = pltpu.roll(u, n - 1, 0)[HALO:HALO + tm]
    return um * cw[0:1] + uc * cw[1:2] + up * cw[2:3]


def _halo_specs(tm, d, n_tokens, axis=0, n_axes=1):
    r = tm // HALO
    last_blk = n_tokens // HALO - 1

    def pick(idx):
        return idx[axis]

    prev = pl.BlockSpec((HALO, d), lambda *idx: (jnp.maximum(pick(idx) * r - 1, 0), 0))
    main = pl.BlockSpec((tm, d), lambda *idx: (pick(idx), 0))
    nxt = pl.BlockSpec((HALO, d), lambda *idx: (jnp.minimum((pick(idx) + 1) * r, last_blk), 0))
    return prev, main, nxt


def _rope(xn, cos, sin, width):
    lane = lax.broadcasted_iota(jnp.int32, xn.shape, 1)
    first_half = (lane % (2 * ROPE_PAIRS)) < ROPE_PAIRS
    partner = jnp.where(first_half, pltpu.roll(xn, width - ROPE_PAIRS, 1), pltpu.roll(xn, ROPE_PAIRS, 1))
    return xn * cos + partner * sin


def _in_proj_kernel(xp_ref, x_ref, xn_ref, gain_ref, w_ref, cos_ref, sin_ref, segq_ref, segk_ref,
                    qg_ref, kg_ref, cw_ref, wvt_ref,
                    qt_out, k_out, vt_out, qb_out, kbt_out, vb_out, ob_out, g_out, *, tiles_per_seq):
    i = pl.program_id(0)
    tm = x_ref.shape[0]
    first = (i % tiles_per_seq) == 0
    last = (i % tiles_per_seq) == tiles_per_seq - 1
    hext = _normed_window(xp_ref, x_ref, xn_ref, gain_ref[...], first, last)
    h = hext[HALO:HALO + tm]

    cos2 = cos_ref[...]
    sin2 = sin_ref[...]
    qa = _dot(h, w_ref[:, _O_QA:_O_KA])
    ka = _dot(h, w_ref[:, _O_KA:_O_VA])
    ms = _dot((qa * qa).astype(BF16), segq_ref[...])
    msk = _dot((ka * ka).astype(BF16), segk_ref[...])
    u = _dot(hext, w_ref[:, _O_QKB:_O_VB])

    qn = qa * lax.rsqrt(ms + EPS) * qg_ref[...]
    cos = jnp.concatenate([cos2] * (D_A // LANES), axis=1)
    sin = jnp.concatenate([sin2] * (D_A // LANES), axis=1)
    qt_out[...] = (_rope(qn, cos, sin, D_A) * Q_SCALE).T.astype(BF16)
    kn = ka * lax.rsqrt(msk + EPS) * kg_ref[...]
    k_out[...] = _rope(kn, cos2, sin2, D_KV_A).astype(BF16)

    c = _conv3(u, cw_ref[...], tm)
    act = c * _sigmoid(c)
    qb_out[...] = act[:, :D_B].astype(BF16)
    kbt_out[...] = (act[:, D_B:] * (HEAD_DIM_B ** -0.5)).T.astype(BF16)

    vb_out[...] = _dot(h, w_ref[:, _O_VB:_O_OB]).astype(BF16)
    ob_out[...] = _dot(h, w_ref[:, _O_OB:_O_G]).astype(BF16)
    g_out[...] = _dot(h, w_ref[:, _O_G:IN_COLS_PAD])
    vt = lax.dot_general(wvt_ref[...], h, (((1,), (1,)), ((), ())), preferred_element_type=F32)
    vt_out[0] = vt.astype(BF16)


def _in_proj(x2d, seq, tm, gain, w_pad, cos, sin, segq, segk, qg, kg, cw, wvt):
    n_tok = x2d.shape[0]
    nt = n_tok // tm
    tps = seq // tm
    prev, main, nxt = _halo_specs(tm, D_MODEL, n_tok)
    const = lambda shape: pl.BlockSpec(shape, lambda i: (0,) * len(shape))
    rope_spec = pl.BlockSpec((tm, LANES), lambda i: (i % tps, 0))
    tok = lambda d: pl.BlockSpec((tm, d), lambda i: (i, 0))
    out_shape = (
        jax.ShapeDtypeStruct((D_A, n_tok), BF16),
        jax.ShapeDtypeStruct((n_tok, D_KV_A), BF16),
        jax.ShapeDtypeStruct((nt, D_KV_A, tm), BF16),
        jax.ShapeDtypeStruct((n_tok, D_B), BF16),
        jax.ShapeDtypeStruct((D_B, n_tok), BF16),
        jax.ShapeDtypeStruct((n_tok, D_B), BF16),
        jax.ShapeDtypeStruct((n_tok, D_B), BF16),
        jax.ShapeDtypeStruct((n_tok, LANES), F32),
    )
    out_specs = (pl.BlockSpec((D_A, tm), lambda i: (0, i)), tok(D_KV_A),
                 pl.BlockSpec((1, D_KV_A, tm), lambda i: (i, 0, 0)),
                 tok(D_B), pl.BlockSpec((D_B, tm), lambda i: (0, i)), tok(D_B), tok(D_B), tok(LANES))
    return pl.pallas_call(
        functools.partial(_in_proj_kernel, tiles_per_seq=tps),
        grid=(nt,),
        in_specs=[prev, main, nxt, const((1, D_MODEL)), const((D_MODEL, IN_COLS_PAD)), rope_spec, rope_spec,
                  const((D_A, D_A)), const((D_KV_A, D_KV_A)), const((1, D_A)), const((1, D_KV_A)),
                  const((3, 2 * D_B)), const((D_KV_A, D_MODEL))],
        out_specs=out_specs,
        out_shape=out_shape,
        compiler_params=_params(1),
        name="in_proj",
    )(x2d, x2d, x2d, gain, w_pad, cos, sin, segq, segk, qg, kg, cw, wvt)


def _attn_kernel(qt_ref, k_ref, vt_ref, o_ref, s_a, s_b, mx_a, mx_b, *, n_chunks, kc):
    tq = qt_ref.shape[1]
    dh = HEAD_DIM_A
    group = N_HEADS_A // N_KV_HEADS_A
    zeros = jnp.zeros((dh, tq), BF16)
    ones = jnp.ones((ONES_ROWS, kc), BF16)
    rhs = []
    for h in range(N_HEADS_A):
        qh = qt_ref[h * dh:(h + 1) * dh, :]
        rhs.append(jnp.concatenate([qh, zeros] if h // group == 0 else [zeros, qh], axis=0))

    def score1(c, j, s_scr, mx_scr):
        kblk = k_ref[pl.ds(pl.multiple_of(c * kc, kc), kc), :]
        s = _dot(kblk, rhs[j])
        s_scr[j] = s
        mx_scr[j] = jnp.max(s, axis=0, keepdims=True)

    def update1(c, j, s_scr, mx_scr, state):
        g = j // group
        vblk = jnp.concatenate([vt_ref[c, g * dh:(g + 1) * dh, :], ones], axis=0)
        m, acc = state
        m_new = jnp.maximum(m, mx_scr[j])
        p = jnp.exp2(s_scr[j] - m_new)
        acc = jnp.exp2(m - m_new) * acc + _dot(vblk, p.astype(BF16))
        return m_new, acc

    def scores(c, s_scr, mx_scr):
        for j in range(N_HEADS_A):
            score1(c, j, s_scr, mx_scr)

    def update(c, s_scr, mx_scr, carry):
        return tuple(update1(c, j, s_scr, mx_scr, state) for j, state in enumerate(carry))

    def fused(c_next, s_next, mx_next, c, s_cur, mx_cur, carry):
        out = []
        for j, state in enumerate(carry):
            score1(c_next, j, s_next, mx_next)
            out.append(update1(c, j, s_cur, mx_cur, state))
        return tuple(out)

    def body(i, carry):
        c = 2 * i
        carry = fused(c + 1, s_b, mx_b, c, s_a, mx_a, carry)
        return fused(c + 2, s_a, mx_a, c + 1, s_b, mx_b, carry)

    init = (jnp.full((1, tq), -jnp.inf, F32), jnp.zeros((dh + ONES_ROWS, tq), F32))
    scores(0, s_a, mx_a)
    carry = lax.fori_loop(0, n_chunks // 2 - 1, body, (init,) * N_HEADS_A)
    carry = fused(n_chunks - 1, s_b, mx_b, n_chunks - 2, s_a, mx_a, carry)
    carry = update(n_chunks - 1, s_b, mx_b, carry)
    for pair in range(N_HEADS_A // 2):
        ot = jnp.concatenate([acc[:dh] / acc[dh:dh + 1] for (_, acc) in carry[2 * pair:2 * pair + 2]],
                             axis=0)
        o_ref[:, pair * LANES:(pair + 1) * LANES] = ot.T.astype(BF16)


def _attention(qt, k, vt, batch, seq, tq):
    n_tok = k.shape[0]
    kc = vt.shape[2]
    nq = seq // tq
    n_chunks = seq // kc
    assert n_chunks % 2 == 0, "the key loop handles chunks in pairs"
    return pl.pallas_call(
        functools.partial(_attn_kernel, n_chunks=n_chunks, kc=kc),
        grid=(batch, nq),
        in_specs=[pl.BlockSpec((D_A, tq), lambda b, i: (0, b * nq + i)),
                  pl.BlockSpec((seq, D_KV_A), lambda b, i: (b, 0)),
                  pl.BlockSpec((n_chunks, D_KV_A, kc), lambda b, i: (b, 0, 0))],
        out_specs=pl.BlockSpec((tq, D_A), lambda b, i: (b * nq + i, 0)),
        out_shape=jax.ShapeDtypeStruct((n_tok, D_A), BF16),
        scratch_shapes=[pltpu.VMEM((N_HEADS_A, kc, tq), F32), pltpu.VMEM((N_HEADS_A, kc, tq), F32),
                        pltpu.VMEM((N_HEADS_A, 1, tq), F32), pltpu.VMEM((N_HEADS_A, 1, tq), F32)],
        compiler_params=_params(2),
        name="attention",
    )(qt, k, vt)


def _log_sigmoid(x):
    return jnp.minimum(x, 0.0) - jnp.log1p(jnp.exp(-jnp.abs(x)))


def _split3(x):
    hi = x.astype(BF16)
    r1 = x - hi.astype(F32)
    mid = r1.astype(BF16)
    lo = (r1 - mid.astype(F32)).astype(BF16)
    return hi, mid, lo


GATE_ROWS = 2 * N_GATES_B


def _gate_prep_kernel(g_ref, bias_ref, bc_out, w0_out, rows_out, stat_out):
    tg = g_ref.shape[0]
    L = CHUNK_B
    n_chunk = tg // L
    gcol = g_ref[...] + bias_ref[...]
    lf = _log_sigmoid(gcol)
    r = lax.broadcasted_iota(jnp.int32, (tg, tg), 0)
    c = lax.broadcasted_iota(jnp.int32, (tg, tg), 1)
    log2_l = L.bit_length() - 1
    same = lax.shift_right_logical(r, log2_l) == lax.shift_right_logical(c, log2_l)
    lower = jnp.where(jnp.logical_and(same, c <= r), 1.0, 0.0).astype(BF16)
    pre = sum(_dot(lower, piece) for piece in _split3(lf))
    tot = jnp.broadcast_to(pre.reshape(n_chunk, L, LANES)[:, L - 1:L, :], (n_chunk, L, LANES)).reshape(tg, LANES)
    suf = tot - pre + lf
    fwd_lane = lax.broadcasted_iota(jnp.int32, (tg, LANES), 1) < 2 * N_HEADS_B
    bc = jnp.where(fwd_lane, pre, suf)
    to_input_lanes = lambda x: pltpu.roll(x, LANES - N_HEADS_B, 1)
    tot_i = to_input_lanes(tot)
    w_log = (tot_i - to_input_lanes(bc) + gcol).reshape(n_chunk, L, LANES)
    wmax = jnp.max(w_log, axis=1, keepdims=True)
    bc_out[...] = bc
    w0_out[...] = jnp.exp(w_log - wmax).reshape(tg, LANES)
    rows_out[0:N_GATES_B, :] = gcol.T[0:N_GATES_B]
    rows_out[N_GATES_B:GATE_ROWS, :] = bc.T[0:N_GATES_B]
    stat_out[:, 0:1, :] = tot_i.reshape(n_chunk, L, LANES)[:, 0:1, :]
    stat_out[:, 1:2, :] = wmax
    stat_out[:, 2:SUBLANES, :] = jnp.zeros((n_chunk, SUBLANES - 2, LANES), F32)


def _gate_prep(g, bias_row, tg):
    n_tok = g.shape[0]
    per = tg // CHUNK_B
    tok = pl.BlockSpec((tg, LANES), lambda i: (i, 0))
    return pl.pallas_call(
        _gate_prep_kernel,
        grid=(n_tok // tg,),
        in_specs=[tok, pl.BlockSpec((1, LANES), lambda i: (0, 0))],
        out_specs=(tok, tok, pl.BlockSpec((GATE_ROWS, tg), lambda i: (0, i)),
                   pl.BlockSpec((per, SUBLANES, LANES), lambda i: (i, 0, 0))),
        out_shape=(jax.ShapeDtypeStruct((n_tok, LANES), F32), jax.ShapeDtypeStruct((n_tok, LANES), F32),
                   jax.ShapeDtypeStruct((GATE_ROWS, n_tok), F32),
                   jax.ShapeDtypeStruct((n_tok // CHUNK_B, SUBLANES, LANES), F32)),
        compiler_params=_params(1),
        name="gate_prep",
    )(g, bias_row)


def _mlstm_kernel(qf_ref, ktf_ref, vf_ref, bcf_ref, w0f_ref, grf_ref, stf_ref,
                  qb_ref, ktb_ref, vb_ref, bcb_ref, w0b_ref, grb_ref, stb_ref,
                  hf_out, hb_out, ct_scr, m_scr, *, cps):
    j = pl.program_id(1)
    L = CHUNK_B
    dh = HEAD_DIM_B

    @pl.when(j == 0)
    def _():
        ct_scr[...] = jnp.zeros_like(ct_scr)
        m_scr[...] = jnp.zeros_like(m_scr)

    row = lax.broadcasted_iota(jnp.int32, (L, L), 0)
    col = lax.broadcasted_iota(jnp.int32, (L, L), 1)
    lower = col <= row
    upper = col >= row
    ones = jnp.ones((L, dh), BF16)
    sel_row = lax.broadcasted_iota(jnp.int32, (LANES, LANES), 0)

    dirs = ((qf_ref, ktf_ref, vf_ref, bcf_ref, w0f_ref, grf_ref, stf_ref, hf_out),
            (qb_ref, ktb_ref, vb_ref, bcb_ref, w0b_ref, grb_ref, stb_ref, hb_out))
    units = []
    for cc in range(cps):
        for d, (q_ref, kt_ref, v_ref, bc_ref, w0_ref, gr_ref, st_ref, out_ref) in enumerate(dirs):
            ci = cc if d == 0 else cps - 1 - cc
            rows = slice(ci * L, (ci + 1) * L)
            bc_pieces = _split3(bc_ref[rows, :])
            w0_all = w0_ref[rows, :].astype(BF16)
            stat_pieces = _split3(st_ref[ci])
            for hd in range(N_HEADS_B):
                u = dict(unit=d * N_HEADS_B + hd, mask=lower if d == 0 else upper, out_ref=out_ref, rows=rows,
                         lanes=slice(hd * dh, (hd + 1) * dh))
                ci_col = d * 2 * N_HEADS_B + hd
                cf_col = ci_col + N_HEADS_B
                pick_f = jnp.where(sel_row == cf_col, 1.0, 0.0).astype(BF16)
                pick_i = jnp.where(sel_row == ci_col, 1.0, 0.0).astype(BF16)
                u["b_c"] = sum(_dot(piece, pick_f) for piece in bc_pieces)
                stat = sum(_dot(piece, pick_i) for piece in stat_pieces)
                u["b_last"], u["wmax"] = stat[0:1], stat[1:2]
                u["w0"] = _dot(w0_all, pick_i)
                u["i_r"] = gr_ref[ci_col:ci_col + 1, rows]
                u["b_r"] = gr_ref[N_GATES_B + cf_col:N_GATES_B + cf_col + 1, rows]
                u["q"] = q_ref[rows, u["lanes"]]
                u["kt"] = kt_ref[u["lanes"], rows]
                u["v_aug"] = jnp.concatenate([v_ref[rows, u["lanes"]], ones], axis=1)
                u["qk"] = _dot(u["q"], u["kt"])
                units.append(u)

    for u in units:
        log_d = jnp.where(u["mask"], u["b_c"] - u["b_r"] + u["i_r"], -jnp.inf)
        u["a"] = jnp.max(log_d, axis=-1, keepdims=True)
        u["p"] = (u["qk"] * jnp.exp(log_d - u["a"])).astype(BF16)
        u["wv"] = (jnp.concatenate([u["w0"], u["w0"]], axis=1) * u["v_aug"].astype(F32)).astype(BF16)
    for u in units:
        u["intra"] = _dot(u["p"], u["v_aug"])
        u["upd"] = _dot(u["kt"], u["wv"])

    for u in units:
        m = m_scr[u["unit"]]
        ct = ct_scr[u["unit"]]
        inter = u["b_c"] + m
        m_row = jnp.maximum(u["a"], inter)
        w_intra = jnp.exp(u["a"] - m_row)
        w_inter = jnp.exp(inter - m_row)
        cross = _dot(u["q"], ct.astype(BF16))
        den = w_inter * cross[:, dh:] + w_intra * u["intra"][:, dh:]
        inv = 1.0 / jnp.maximum(jnp.abs(den), jnp.exp(-m_row))
        u["out_ref"][u["rows"], u["lanes"]] = ((w_inter * inv) * cross[:, :dh]
                                               + (w_intra * inv) * u["intra"][:, :dh]).astype(BF16)
        m_new = jnp.maximum(u["b_last"] + m, u["wmax"])
        decay = jnp.exp(u["b_last"] + m - m_new)
        gain = jnp.exp(u["wmax"] - m_new)
        ct_scr[u["unit"]] = (jnp.concatenate([decay, decay], axis=1) * ct
                             + jnp.concatenate([gain, gain], axis=1) * u["upd"])
        m_scr[u["unit"]] = m_new


def _mlstm(qb, kbt, vb, bc, w0, grows, stat, batch, seq, cps):
    n_tok = qb.shape[0]
    rows = cps * CHUNK_B
    ns = seq // rows
    fwd = lambda b, j: b * ns + j
    bwd = lambda b, j: b * ns + ns - 1 - j

    def specs(blk):
        return [pl.BlockSpec((rows, D_B), lambda b, j: (blk(b, j), 0)),
                pl.BlockSpec((D_B, rows), lambda b, j: (0, blk(b, j))),
                pl.BlockSpec((rows, D_B), lambda b, j: (blk(b, j), 0)),
                pl.BlockSpec((rows, LANES), lambda b, j: (blk(b, j), 0)),
                pl.BlockSpec((rows, LANES), lambda b, j: (blk(b, j), 0)),
                pl.BlockSpec((GATE_ROWS, rows), lambda b, j: (0, blk(b, j))),
                pl.BlockSpec((cps, SUBLANES, LANES), lambda b, j: (blk(b, j), 0, 0))]

    out = lambda blk: pl.BlockSpec((rows, D_B), lambda b, j: (blk(b, j), 0))
    n_units = 2 * N_HEADS_B
    args = (qb, kbt, vb, bc, w0, grows, stat)
    return pl.pallas_call(
        functools.partial(_mlstm_kernel, cps=cps),
        grid=(batch, ns),
        in_specs=specs(fwd) + specs(bwd),
        out_specs=(out(fwd), out(bwd)),
        out_shape=(jax.ShapeDtypeStruct((n_tok, D_B), BF16), jax.ShapeDtypeStruct((n_tok, D_B), BF16)),
        scratch_shapes=[pltpu.VMEM((n_units, HEAD_DIM_B, 2 * HEAD_DIM_B), F32),
                        pltpu.VMEM((n_units, 1, LANES), F32)],
        compiler_params=_params(2),
        name="mlstm",
    )(*args, *args)


def _out_proj_kernel(x_ref, oa_ref, hf_ref, hb_ref, ob_ref, hg_ref, w_ref, o_ref):
    hsum = hf_ref[...].astype(F32) + hb_ref[...].astype(F32)
    parts = []
    for hd in range(N_HEADS_B):
        hh = hsum[:, hd * HEAD_DIM_B:(hd + 1) * HEAD_DIM_B]
        parts.append(hh * lax.rsqrt(jnp.mean(hh * hh, axis=-1, keepdims=True) + EPS))
    hn = jnp.concatenate(parts, axis=1) * hg_ref[...]
    out_b = (hn * _sigmoid(ob_ref[...].astype(F32))).astype(BF16)
    o_ref[...] = x_ref[...] + _dot(oa_ref[...], w_ref[:D_A]) + _dot(out_b, w_ref[D_A:])


def _out_proj(x2d, out_a, hf, hb, ob, hg, w, tm):
    n_tok = x2d.shape[0]
    tok = lambda d: pl.BlockSpec((tm, d), lambda i: (i, 0))
    return pl.pallas_call(
        _out_proj_kernel,
        grid=(n_tok // tm,),
        in_specs=[tok(D_MODEL), tok(D_A), tok(D_B), tok(D_B), tok(D_B),
                  pl.BlockSpec((1, D_B), lambda i: (0, 0)),
                  pl.BlockSpec((D_A + D_B, D_MODEL), lambda i: (0, 0))],
        out_specs=tok(D_MODEL),
        out_shape=jax.ShapeDtypeStruct((n_tok, D_MODEL), F32),
        compiler_params=_params(1),
        name="out_proj",
    )(x2d, out_a, hf, hb, ob, hg, w)


MXU_COLS = 256


def _ffn_kernel(xp_ref, x_ref, xn_ref, gain_ref, wu_ref, cw_ref, cb_ref, wd_ref, o_ref, *, tiles_per_seq):
    i = pl.program_id(0)
    tm = x_ref.shape[0]
    first = (i % tiles_per_seq) == 0
    last = (i % tiles_per_seq) == tiles_per_seq - 1
    hext = _normed_window(xp_ref, x_ref, xn_ref, gain_ref[...], first, last)
    acc = x_ref[...]
    for lo, hi in ((0, 5 * MXU_COLS), (5 * MXU_COLS, D_FF)):
        gcols = slice(lo, hi)
        vcols = slice(D_FF + lo, D_FF + hi)
        gate = _conv3(_dot(hext, wu_ref[:, gcols]), cw_ref[:, gcols], tm) + cb_ref[:, gcols]
        val = _conv3(_dot(hext, wu_ref[:, vcols]), cw_ref[:, vcols], tm) + cb_ref[:, vcols]
        acc = acc + _dot((gate * _sigmoid(gate) * val).astype(BF16), wd_ref[lo:hi, :])
    o_ref[...] = acc


def _ffn(x2d, seq, tm, gain, w_up, cw, cb, w_down):
    n_tok = x2d.shape[0]
    prev, main, nxt = _halo_specs(tm, D_MODEL, n_tok)
    full = lambda r: pl.BlockSpec((r, 2 * D_FF), lambda i: (0, 0))
    once = pl.Buffered(1)
    return pl.pallas_call(
        functools.partial(_ffn_kernel, tiles_per_seq=seq // tm),
        grid=(n_tok // tm,),
        in_specs=[prev, main, nxt, pl.BlockSpec((1, D_MODEL), lambda i: (0, 0)),
                  pl.BlockSpec((D_MODEL, 2 * D_FF), lambda i: (0, 0), pipeline_mode=once), full(3), full(1),
                  pl.BlockSpec((D_FF, D_MODEL), lambda i: (0, 0), pipeline_mode=once)],
        out_specs=pl.BlockSpec((tm, D_MODEL), lambda i: (i, 0)),
        out_shape=jax.ShapeDtypeStruct((n_tok, D_MODEL), F32),
        compiler_params=_params(1),
        name="ffn",
    )(x2d, x2d, x2d, gain, w_up, cw, cb, w_down)


def _conf_glu_kernel(x_ref, gain_ref, w_ref, b_ref, o_ref):
    h = _rms(x_ref[...], gain_ref[...]).astype(BF16)
    u = _dot(h, w_ref[...]) + b_ref[...]
    o_ref[...] = u[:, :D_MODEL] * _sigmoid(u[:, D_MODEL:])


def _conf_glu(x2d, gain, w, b, tm):
    n_tok = x2d.shape[0]
    return pl.pallas_call(
        _conf_glu_kernel,
        grid=(n_tok // tm,),
        in_specs=[pl.BlockSpec((tm, D_MODEL), lambda i: (i, 0)), pl.BlockSpec((1, D_MODEL), lambda i: (0, 0)),
                  pl.BlockSpec((D_MODEL, 2 * D_MODEL), lambda i: (0, 0)),
                  pl.BlockSpec((1, 2 * D_MODEL), lambda i: (0, 0))],
        out_specs=pl.BlockSpec((tm, D_MODEL), lambda i: (i, 0)),
        out_shape=jax.ShapeDtypeStruct((n_tok, D_MODEL), F32),
        compiler_params=_params(1),
        name="conf_glu",
    )(x2d, gain, w, b)


CONV_ROWS = 64


def _conf_conv_kernel(up_ref, u_ref, un_ref, x_ref, wdw_ref, bdw_ref, lng_ref, lnb_ref, w2_ref, b2_ref, o_ref,
                      rot_scr, act_scr, *, tiles_per_seq):
    i = pl.program_id(0)
    tm = u_ref.shape[0]
    n = tm + 2 * HALO
    first = (i % tiles_per_seq) == 0
    last = (i % tiles_per_seq) == tiles_per_seq - 1
    win = jnp.concatenate([jnp.where(first, 0.0, up_ref[...]), u_ref[...], jnp.where(last, 0.0, un_ref[...])], axis=0)
    rot_scr[0] = win
    for r in range(1, 8):
        rot_scr[r] = pltpu.roll(win, n - r, 0)

    for c in range(D_MODEL // LANES):
        lanes = slice(c * LANES, (c + 1) * LANES)
        taps = [wdw_ref[k, :, lanes] for k in range(CONV_C)]

        def block(rb, carry, lanes=lanes, taps=taps):
            r0 = pl.multiple_of(rb * CONV_ROWS, CONV_ROWS)
            for a in range(CONV_ROWS // SUBLANES):
                acc = None
                for k in range(CONV_C):
                    shift = k + 1
                    rows = pl.ds(r0 + (shift // SUBLANES + a) * SUBLANES, SUBLANES)
                    term = taps[k] * rot_scr[shift % SUBLANES, rows, lanes]
                    acc = term if acc is None else acc + term
                act_scr[pl.ds(r0 + a * SUBLANES, SUBLANES), lanes] = acc
            return carry

        lax.fori_loop(0, tm // CONV_ROWS, block, 0)
    conv = act_scr[...] + bdw_ref[...]
    xc = conv - jnp.mean(conv, axis=-1, keepdims=True)
    y = xc * lax.rsqrt(jnp.mean(xc * xc, axis=-1, keepdims=True) + EPS) * lng_ref[...] + lnb_ref[...]
    act = (y * _sigmoid(y)).astype(BF16)
    o_ref[...] = x_ref[...] + _dot(act, w2_ref[...]) + b2_ref[...]


def _conf_conv(u, x2d, seq, tm, wdw, bdw, lng, lnb, w2, b2):
    n_tok = x2d.shape[0]
    prev, main, nxt = _halo_specs(tm, D_MODEL, n_tok)
    vec = pl.BlockSpec((1, D_MODEL), lambda i: (0, 0))
    return pl.pallas_call(
        functools.partial(_conf_conv_kernel, tiles_per_seq=seq // tm),
        grid=(n_tok // tm,),
        in_specs=[prev, main, nxt, pl.BlockSpec((tm, D_MODEL), lambda i: (i, 0)),
                  pl.BlockSpec((CONV_C, SUBLANES, D_MODEL), lambda i: (0, 0, 0)), vec, vec, vec,
                  pl.BlockSpec((D_MODEL, D_MODEL), lambda i: (0, 0)), vec],
        out_specs=pl.BlockSpec((tm, D_MODEL), lambda i: (i, 0)),
        out_shape=jax.ShapeDtypeStruct((n_tok, D_MODEL), F32),
        scratch_shapes=[pltpu.VMEM((8, tm + 2 * HALO, D_MODEL), F32), pltpu.VMEM((tm, D_MODEL), F32)],
        compiler_params=_params(1),
        name="conf_conv",
    )(u, u, u, x2d, wdw, bdw, lng, lnb, w2, b2)


def _rope_tables(seq):
    pos = jnp.arange(seq)
    inv = ROPE_THETA ** (-jnp.arange(ROPE_PAIRS, dtype=F32) / ROPE_PAIRS)
    lane = np.arange(HEAD_DIM_A)
    section, half, pair = lane // (2 * ROPE_PAIRS), (lane // ROPE_PAIRS) % 2, lane % ROPE_PAIRS
    row_idx = (pos // GRID_W).astype(F32)[:, None]
    col_idx = (pos % GRID_W).astype(F32)[:, None]
    ang = jnp.where(jnp.asarray(section == 0)[None, :], row_idx, col_idx) * inv[pair][None, :]
    sign = jnp.asarray(np.where(half == 0, -1.0, 1.0), F32)[None, :]
    return jnp.tile(jnp.cos(ang), (1, 2)), jnp.tile(jnp.sin(ang) * sign, (1, 2))


def _segment_mean_matrix(n, width):
    seg = np.arange(n) // width
    return jnp.asarray((seg[:, None] == seg[None, :]).astype(np.float32) / width, BF16)


def _tile_size(seq, want):
    return min(want, seq)


def _trunk(x, p):
    batch, seq, _ = x.shape
    n_tok = batch * seq
    x2d = x.reshape(n_tok, D_MODEL)
    tm = _tile_size(seq, 512)
    cps = 4 if seq % (4 * CHUNK_B) == 0 else 2

    qt, k, vt, qb, kbt, vb, ob, g = _in_proj(x2d, seq, tm, p["mix_norm_e"], p["w_in"], p["cos"], p["sin"],
                                            p["segq"], p["segk"], p["qg"], p["kg"], p["w_qk_conv"], p["w_v_t"])
    out_a = _attention(qt, k, vt, batch, seq, _tile_size(seq, 512))
    bc, w0, grows, stat = _gate_prep(g, p["b_gates_row"], tm)
    hf, hb = _mlstm(qb, kbt, vb, bc, w0, grows, stat, batch, seq, cps)
    x2d = _out_proj(x2d, out_a, hf, hb, ob, p["h_gain"], p["w_out"], tm)
    x2d = _ffn(x2d, seq, tm, p["ffn_norm0"], p["w_up0"], p["w_dw_ff0"], p["b_dw_ff0"], p["w_down0"])
    u = _conf_glu(x2d, p["mix_norm_o"], p["w_pw1"], p["b_pw1"], tm)
    x2d = _conf_conv(u, x2d, seq, tm, p["w_dw_c"], p["b_dw_c"], p["ln_g"], p["ln_b"],
                     p["w_pw2"], p["b_pw2"])
    x2d = _ffn(x2d, seq, tm, p["ffn_norm1"], p["w_up1"], p["w_dw_ff1"], p["b_dw_ff1"], p["w_down1"])
    return x2d.reshape(batch, seq, D_MODEL)


def kernel(x_prompt, x_sample, mix_norm_e, w_in, q_gain_a, k_gain_a, w_qk_conv_b, b_gates_b, h_gain_b, w_out_e, mix_norm_o, w_pw1_c, b_pw1_c, w_dw_c, b_dw_c, ln_g_c, ln_b_c, w_pw2_c, b_pw2_c, ffn_norm, w_up, w_dw_ff, b_dw_ff, w_down):
    row = lambda a: a.reshape(1, -1).astype(F32)
    p = {
        "mix_norm_e": row(mix_norm_e[0]),
        "w_in": jnp.pad(w_in[0].astype(BF16), ((0, 0), (0, IN_COLS_PAD - IN_COLS))),
        "segq": _segment_mean_matrix(D_A, HEAD_DIM_A),
        "segk": _segment_mean_matrix(D_KV_A, HEAD_DIM_A),
        "qg": row(jnp.tile(q_gain_a[0], N_HEADS_A)),
        "kg": row(jnp.tile(k_gain_a[0], N_KV_HEADS_A)),
        "w_qk_conv": w_qk_conv_b[0].astype(F32),
        "w_v_t": w_in[0][:, _O_VA:_O_QKB].T.astype(BF16),
        "b_gates_row": jnp.pad(row(b_gates_b[0]), ((0, 0), (0, LANES - N_GATES_B))),
        "h_gain": row(h_gain_b[0]),
        "w_out": w_out_e[0].astype(BF16),
        "mix_norm_o": row(mix_norm_o[0]),
        "w_pw1": w_pw1_c[0].astype(BF16),
        "b_pw1": row(b_pw1_c[0]),
        "w_dw_c": jnp.broadcast_to(w_dw_c[0].astype(F32)[:, None, :], (CONV_C, SUBLANES, D_MODEL)),
        "b_dw_c": row(b_dw_c[0]),
        "ln_g": row(ln_g_c[0]),
        "ln_b": row(ln_b_c[0]),
        "w_pw2": w_pw2_c[0].astype(BF16),
        "b_pw2": row(b_pw2_c[0]),
    }
    for layer in range(2):
        p[f"ffn_norm{layer}"] = row(ffn_norm[layer])
        p[f"w_up{layer}"] = w_up[layer].astype(BF16)
        p[f"w_dw_ff{layer}"] = w_dw_ff[layer].astype(F32)
        p[f"b_dw_ff{layer}"] = row(b_dw_ff[layer])
        p[f"w_down{layer}"] = w_down[layer].astype(BF16)
    outs = []
    for x in (x_prompt, x_sample):
        p["cos"], p["sin"] = _rope_tables(x.shape[1])
        outs.append(_trunk(x, p))
    return tuple(outs)
```

```python
import functools

import numpy as np
import jax
import jax.numpy as jnp
from jax import lax
from jax.experimental import pallas as pl
from jax.experimental.pallas import tpu as pltpu

D_MODEL = 1024
GRID_W = 64
N_HEADS_A = 8
N_KV_HEADS_A = 2
HEAD_DIM_A = 64
D_A = N_HEADS_A * HEAD_DIM_A
D_KV_A = N_KV_HEADS_A * HEAD_DIM_A
ROPE_THETA = 10000.0
ROPE_PAIRS = HEAD_DIM_A // 4
N_HEADS_B = 4
HEAD_DIM_B = 128
D_B = N_HEADS_B * HEAD_DIM_B
CHUNK_B = 128
N_GATES_B = 4 * N_HEADS_B
IN_COLS = D_A + 2 * D_KV_A + 4 * D_B + N_GATES_B
IN_COLS_PAD = 2944
CONV_C = 31
D_FF = 2816
EPS = 1e-6
Q_SCALE = HEAD_DIM_A ** -0.5 * float(np.log2(np.e))

LANES = 128
SUBLANES = 8
HALO = 16
ONES_ROWS = 16
VMEM_LIMIT = 56 * 1024 * 1024

F32 = jnp.float32
BF16 = jnp.bfloat16

_O_QA, _O_KA, _O_VA, _O_QKB, _O_VB, _O_OB, _O_G = 0, 512, 640, 768, 1792, 2304, 2816


def _params(n_axes):
    return pltpu.CompilerParams(dimension_semantics=("arbitrary",) * n_axes,
                                vmem_limit_bytes=VMEM_LIMIT)


def _dot(a, b):
    return jnp.dot(a, b, preferred_element_type=F32)


def _rms(x, gain):
    ms = jnp.mean(x * x, axis=-1, keepdims=True)
    return x * lax.rsqrt(ms + EPS) * gain


def _sigmoid(x):
    return 1.0 / (1.0 + jnp.exp(-x))


def _normed_window(xp_ref, x_ref, xn_ref, gain, first, last):
    hp = jnp.where(first, 0.0, _rms(xp_ref[...], gain))
    hn = jnp.where(last, 0.0, _rms(xn_ref[...], gain))
    h = _rms(x_ref[...], gain)
    return jnp.concatenate([hp, h, hn], axis=0).astype(BF16)


def _conv3(u, cw, tm):
    n = u.shape[0]
    um = pltpu.roll(u, 1, 0)[HALO:HALO + tm]
    uc = u[HALO:HALO + tm]
    up = pltpu.roll(u, n - 1, 0)[HALO:HALO + tm]
    return um * cw[0:1] + uc * cw[1:2] + up * cw[2:3]


def _halo_specs(tm, d, n_tokens, axis=0, n_axes=1):
    r = tm // HALO
    last_blk = n_tokens // HALO - 1

    def pick(idx):
        return idx[axis]

    prev = pl.BlockSpec((HALO, d), lambda *idx: (jnp.maximum(pick(idx) * r - 1, 0), 0))
    main = pl.BlockSpec((tm, d), lambda *idx: (pick(idx), 0))
    nxt = pl.BlockSpec((HALO, d), lambda *idx: (jnp.minimum((pick(idx) + 1) * r, last_blk), 0))
    return prev, main, nxt


def _rope(xn, cos, sin, width):
    lane = lax.broadcasted_iota(jnp.int32, xn.shape, 1)
    first_half = (lane % (2 * ROPE_PAIRS)) < ROPE_PAIRS
    partner = jnp.where(first_half, pltpu.roll(xn, width - ROPE_PAIRS, 1), pltpu.roll(xn, ROPE_PAIRS, 1))
    return xn * cos + partner * sin


def _in_proj_kernel(xp_ref, x_ref, xn_ref, gain_ref, w_ref, cos_ref, sin_ref, segq_ref, segk_ref,
                    qg_ref, kg_ref, cw_ref, wvt_ref,
                    qt_out, k_out, vt_out, qb_out, kbt_out, vb_out, ob_out, g_out, *, tiles_per_seq):
    i = pl.program_id(0)
    tm = x_ref.shape[0]
    first = (i % tiles_per_seq) == 0
    last = (i % tiles_per_seq) == tiles_per_seq - 1
    hext = _normed_window(xp_ref, x_ref, xn_ref, gain_ref[...], first, last)
    h = hext[HALO:HALO + tm]

    cos2 = cos_ref[...]
    sin2 = sin_ref[...]
    qa = _dot(h, w_ref[:, _O_QA:_O_KA])
    ka = _dot(h, w_ref[:, _O_KA:_O_VA])
    ms = _dot((qa * qa).astype(BF16), segq_ref[...])
    msk = _dot((ka * ka).astype(BF16), segk_ref[...])
    u = _dot(hext, w_ref[:, _O_QKB:_O_VB])

    qn = qa * lax.rsqrt(ms + EPS) * qg_ref[...]
    cos = jnp.concatenate([cos2] * (D_A // LANES), axis=1)
    sin = jnp.concatenate([sin2] * (D_A // LANES), axis=1)
    qt_out[...] = (_rope(qn, cos, sin, D_A) * Q_SCALE).T.astype(BF16)
    kn = ka * lax.rsqrt(msk + EPS) * kg_ref[...]
    k_out[...] = _rope(kn, cos2, sin2, D_KV_A).astype(BF16)

    c = _conv3(u, cw_ref[...], tm)
    act = c * _sigmoid(c)
    qb_out[...] = act[:, :D_B].astype(BF16)
    kbt_out[...] = (act[:, D_B:] * (HEAD_DIM_B ** -0.5)).T.astype(BF16)

    vb_out[...] = _dot(h, w_ref[:, _O_VB:_O_OB]).astype(BF16)
    ob_out[...] = _dot(h, w_ref[:, _O_OB:_O_G]).astype(BF16)
    g_out[...] = _dot(h, w_ref[:, _O_G:IN_COLS_PAD])
    vt = lax.dot_general(wvt_ref[...], h, (((1,), (1,)), ((), ())), preferred_element_type=F32)
    vt_out[0] = vt.astype(BF16)


def _in_proj(x2d, seq, tm, gain, w_pad, cos, sin, segq, segk, qg, kg, cw, wvt):
    n_tok = x2d.shape[0]
    nt = n_tok // tm
    tps = seq // tm
    prev, main, nxt = _halo_specs(tm, D_MODEL, n_tok)
    const = lambda shape: pl.BlockSpec(shape, lambda i: (0,) * len(shape))
    rope_spec = pl.BlockSpec((tm, LANES), lambda i: (i % tps, 0))
    tok = lambda d: pl.BlockSpec((tm, d), lambda i: (i, 0))
    out_shape = (
        jax.ShapeDtypeStruct((D_A, n_tok), BF16),
        jax.ShapeDtypeStruct((n_tok, D_KV_A), BF16),
        jax.ShapeDtypeStruct((nt, D_KV_A, tm), BF16),
        jax.ShapeDtypeStruct((n_tok, D_B), BF16),
        jax.ShapeDtypeStruct((D_B, n_tok), BF16),
        jax.ShapeDtypeStruct((n_tok, D_B), BF16),
        jax.ShapeDtypeStruct((n_tok, D_B), BF16),
        jax.ShapeDtypeStruct((n_tok, LANES), F32),
    )
    out_specs = (pl.BlockSpec((D_A, tm), lambda i: (0, i)), tok(D_KV_A),
                 pl.BlockSpec((1, D_KV_A, tm), lambda i: (i, 0, 0)),
                 tok(D_B), pl.BlockSpec((D_B, tm), lambda i: (0, i)), tok(D_B), tok(D_B), tok(LANES))
    return pl.pallas_call(
        functools.partial(_in_proj_kernel, tiles_per_seq=tps),
        grid=(nt,),
        in_specs=[prev, main, nxt, const((1, D_MODEL)), const((D_MODEL, IN_COLS_PAD)), rope_spec, rope_spec,
                  const((D_A, D_A)), const((D_KV_A, D_KV_A)), const((1, D_A)), const((1, D_KV_A)),
                  const((3, 2 * D_B)), const((D_KV_A, D_MODEL))],
        out_specs=out_specs,
        out_shape=out_shape,
        compiler_params=_params(1),
        name="in_proj",
    )(x2d, x2d, x2d, gain, w_pad, cos, sin, segq, segk, qg, kg, cw, wvt)


def _attn_kernel(qt_ref, k_ref, vt_ref, o_ref, s_a, s_b, mx_a, mx_b, *, n_chunks, kc):
    tq = qt_ref.shape[1]
    dh = HEAD_DIM_A
    group = N_HEADS_A // N_KV_HEADS_A
    zeros = jnp.zeros((dh, tq), BF16)
    ones = jnp.ones((ONES_ROWS, kc), BF16)
    rhs = []
    for h in range(N_HEADS_A):
        qh = qt_ref[h * dh:(h + 1) * dh, :]
        rhs.append(jnp.concatenate([qh, zeros] if h // group == 0 else [zeros, qh], axis=0))

    def score1(c, j, s_scr, mx_scr):
        kblk = k_ref[pl.ds(pl.multiple_of(c * kc, kc), kc), :]
        s = _dot(kblk, rhs[j])
        s_scr[j] = s
        mx_scr[j] = jnp.max(s, axis=0, keepdims=True)

    def update1(c, j, s_scr, mx_scr, state):
        g = j // group
        vblk = jnp.concatenate([vt_ref[c, g * dh:(g + 1) * dh, :], ones], axis=0)
        m, acc = state
        m_new = jnp.maximum(m, mx_scr[j])
        p = jnp.exp2(s_scr[j] - m_new)
        acc = jnp.exp2(m - m_new) * acc + _dot(vblk, p.astype(BF16))
        return m_new, acc

    def scores(c, s_scr, mx_scr):
        for j in range(N_HEADS_A):
            score1(c, j, s_scr, mx_scr)

    def update(c, s_scr, mx_scr, carry):
        return tuple(update1(c, j, s_scr, mx_scr, state) for j, state in enumerate(carry))

    def fused(c_next, s_next, mx_next, c, s_cur, mx_cur, carry):
        out = []
        for j, state in enumerate(carry):
            score1(c_next, j, s_next, mx_next)
            out.append(update1(c, j, s_cur, mx_cur, state))
        return tuple(out)

    def body(i, carry):
        c = 2 * i
        carry = fused(c + 1, s_b, mx_b, c, s_a, mx_a, carry)
        return fused(c + 2, s_a, mx_a, c + 1, s_b, mx_b, carry)

    init = (jnp.full((1, tq), -jnp.inf, F32), jnp.zeros((dh + ONES_ROWS, tq), F32))
    scores(0, s_a, mx_a)
    carry = lax.fori_loop(0, n_chunks // 2 - 1, body, (init,) * N_HEADS_A)
    carry = fused(n_chunks - 1, s_b, mx_b, n_chunks - 2, s_a, mx_a, carry)
    carry = update(n_chunks - 1, s_b, mx_b, carry)
    for pair in range(N_HEADS_A // 2):
        ot = jnp.concatenate([acc[:dh] / acc[dh:dh + 1] for (_, acc) in carry[2 * pair:2 * pair + 2]],
                             axis=0)
        o_ref[:, pair * LANES:(pair + 1) * LANES] = ot.T.astype(BF16)


def _attention(qt, k, vt, batch, seq, tq):
    n_tok = k.shape[0]
    kc = vt.shape[2]
    nq = seq // tq
    n_chunks = seq // kc
    assert n_chunks % 2 == 0, "the key loop handles chunks in pairs"
    return pl.pallas_call(
        functools.partial(_attn_kernel, n_chunks=n_chunks, kc=kc),
        grid=(batch, nq),
        in_specs=[pl.BlockSpec((D_A, tq), lambda b, i: (0, b * nq + i)),
                  pl.BlockSpec((seq, D_KV_A), lambda b, i: (b, 0)),
                  pl.BlockSpec((n_chunks, D_KV_A, kc), lambda b, i: (b, 0, 0))],
        out_specs=pl.BlockSpec((tq, D_A), lambda b, i: (b * nq + i, 0)),
        out_shape=jax.ShapeDtypeStruct((n_tok, D_A), BF16),
        scratch_shapes=[pltpu.VMEM((N_HEADS_A, kc, tq), F32), pltpu.VMEM((N_HEADS_A, kc, tq), F32),
                        pltpu.VMEM((N_HEADS_A, 1, tq), F32), pltpu.VMEM((N_HEADS_A, 1, tq), F32)],
        compiler_params=_params(2),
        name="attention",
    )(qt, k, vt)


def _log_sigmoid(x):
    return jnp.minimum(x, 0.0) - jnp.log1p(jnp.exp(-jnp.abs(x)))


def _split3(x):
    hi = x.astype(BF16)
    r1 = x - hi.astype(F32)
    mid = r1.astype(BF16)
    lo = (r1 - mid.astype(F32)).astype(BF16)
    return hi, mid, lo


GATE_ROWS = 2 * N_GATES_B


def _gate_prep_kernel(g_ref, bias_ref, bc_out, w0_out, rows_out, stat_out):
    tg = g_ref.shape[0]
    L = CHUNK_B
    n_chunk = tg // L
    gcol = g_ref[...] + bias_ref[...]
    lf = _log_sigmoid(gcol)
    r = lax.broadcasted_iota(jnp.int32, (tg, tg), 0)
    c = lax.broadcasted_iota(jnp.int32, (tg, tg), 1)
    log2_l = L.bit_length() - 1
    same = lax.shift_right_logical(r, log2_l) == lax.shift_right_logical(c, log2_l)
    lower = jnp.where(jnp.logical_and(same, c <= r), 1.0, 0.0).astype(BF16)
    pre = sum(_dot(lower, piece) for piece in _split3(lf))
    tot = jnp.broadcast_to(pre.reshape(n_chunk, L, LANES)[:, L - 1:L, :], (n_chunk, L, LANES)).reshape(tg, LANES)
    suf = tot - pre + lf
    fwd_lane = lax.broadcasted_iota(jnp.int32, (tg, LANES), 1) < 2 * N_HEADS_B
    bc = jnp.where(fwd_lane, pre, suf)
    to_input_lanes = lambda x: pltpu.roll(x, LANES - N_HEADS_B, 1)
    tot_i = to_input_lanes(tot)
    w_log = (tot_i - to_input_lanes(bc) + gcol).reshape(n_chunk, L, LANES)
    wmax = jnp.max(w_log, axis=1, keepdims=True)
    bc_out[...] = bc
    w0_out[...] = jnp.exp(w_log - wmax).reshape(tg, LANES)
    rows_out[0:N_GATES_B, :] = gcol.T[0:N_GATES_B]
    rows_out[N_GATES_B:GATE_ROWS, :] = bc.T[0:N_GATES_B]
    stat_out[:, 0:1, :] = tot_i.reshape(n_chunk, L, LANES)[:, 0:1, :]
    stat_out[:, 1:2, :] = wmax
    stat_out[:, 2:SUBLANES, :] = jnp.zeros((n_chunk, SUBLANES - 2, LANES), F32)


def _gate_prep(g, bias_row, tg):
    n_tok = g.shape[0]
    per = tg // CHUNK_B
    tok = pl.BlockSpec((tg, LANES), lambda i: (i, 0))
    return pl.pallas_call(
        _gate_prep_kernel,
        grid=(n_tok // tg,),
        in_specs=[tok, pl.BlockSpec((1, LANES), lambda i: (0, 0))],
        out_specs=(tok, tok, pl.BlockSpec((GATE_ROWS, tg), lambda i: (0, i)),
                   pl.BlockSpec((per, SUBLANES, LANES), lambda i: (i, 0, 0))),
        out_shape=(jax.ShapeDtypeStruct((n_tok, LANES), F32), jax.ShapeDtypeStruct((n_tok, LANES), F32),
                   jax.ShapeDtypeStruct((GATE_ROWS, n_tok), F32),
                   jax.ShapeDtypeStruct((n_tok // CHUNK_B, SUBLANES, LANES), F32)),
        compiler_params=_params(1),
        name="gate_prep",
    )(g, bias_row)


def _mlstm_kernel(qf_ref, ktf_ref, vf_ref, bcf_ref, w0f_ref, grf_ref, stf_ref,
                  qb_ref, ktb_ref, vb_ref, bcb_ref, w0b_ref, grb_ref, stb_ref,
                  hf_out, hb_out, ct_scr, m_scr, *, cps):
    j = pl.program_id(1)
    L = CHUNK_B
    dh = HEAD_DIM_B

    @pl.when(j == 0)
    def _():
        ct_scr[...] = jnp.zeros_like(ct_scr)
        m_scr[...] = jnp.zeros_like(m_scr)

    row = lax.broadcasted_iota(jnp.int32, (L, L), 0)
    col = lax.broadcasted_iota(jnp.int32, (L, L), 1)
    lower = col <= row
    upper = col >= row
    ones = jnp.ones((L, dh), BF16)
    sel_row = lax.broadcasted_iota(jnp.int32, (LANES, LANES), 0)

    dirs = ((qf_ref, ktf_ref, vf_ref, bcf_ref, w0f_ref, grf_ref, stf_ref, hf_out),
            (qb_ref, ktb_ref, vb_ref, bcb_ref, w0b_ref, grb_ref, stb_ref, hb_out))
    units = []
    for cc in range(cps):
        for d, (q_ref, kt_ref, v_ref, bc_ref, w0_ref, gr_ref, st_ref, out_ref) in enumerate(dirs):
            ci = cc if d == 0 else cps - 1 - cc
            rows = slice(ci * L, (ci + 1) * L)
            bc_pieces = _split3(bc_ref[rows, :])
            w0_all = w0_ref[rows, :].astype(BF16)
            stat_pieces = _split3(st_ref[ci])
            for hd in range(N_HEADS_B):
                u = dict(unit=d * N_HEADS_B + hd, mask=lower if d == 0 else upper, out_ref=out_ref, rows=rows,
                         lanes=slice(hd * dh, (hd + 1) * dh))
                ci_col = d * 2 * N_HEADS_B + hd
                cf_col = ci_col + N_HEADS_B
                pick_f = jnp.where(sel_row == cf_col, 1.0, 0.0).astype(BF16)
                pick_i = jnp.where(sel_row == ci_col, 1.0, 0.0).astype(BF16)
                u["b_c"] = sum(_dot(piece, pick_f) for piece in bc_pieces)
                stat = sum(_dot(piece, pick_i) for piece in stat_pieces)
                u["b_last"], u["wmax"] = stat[0:1], stat[1:2]
                u["w0"] = _dot(w0_all, pick_i)
                u["i_r"] = gr_ref[ci_col:ci_col + 1, rows]
                u["b_r"] = gr_ref[N_GATES_B + cf_col:N_GATES_B + cf_col + 1, rows]
                u["q"] = q_ref[rows, u["lanes"]]
                u["kt"] = kt_ref[u["lanes"], rows]
                u["v_aug"] = jnp.concatenate([v_ref[rows, u["lanes"]], ones], axis=1)
                u["qk"] = _dot(u["q"], u["kt"])
                units.append(u)

    for u in units:
        log_d = jnp.where(u["mask"], u["b_c"] - u["b_r"] + u["i_r"], -jnp.inf)
        u["a"] = jnp.max(log_d, axis=-1, keepdims=True)
        u["p"] = (u["qk"] * jnp.exp(log_d - u["a"])).astype(BF16)
        u["wv"] = (jnp.concatenate([u["w0"], u["w0"]], axis=1) * u["v_aug"].astype(F32)).astype(BF16)
    for u in units:
        u["intra"] = _dot(u["p"], u["v_aug"])
        u["upd"] = _dot(u["kt"], u["wv"])

    for u in units:
        m = m_scr[u["unit"]]
        ct = ct_scr[u["unit"]]
        inter = u["b_c"] + m
        m_row = jnp.maximum(u["a"], inter)
        w_intra = jnp.exp(u["a"] - m_row)
        w_inter = jnp.exp(inter - m_row)
        cross = _dot(u["q"], ct.astype(BF16))
        den = w_inter * cross[:, dh:] + w_intra * u["intra"][:, dh:]
        inv = 1.0 / jnp.maximum(jnp.abs(den), jnp.exp(-m_row))
        u["out_ref"][u["rows"], u["lanes"]] = ((w_inter * inv) * cross[:, :dh]
                                               + (w_intra * inv) * u["intra"][:, :dh]).astype(BF16)
        m_new = jnp.maximum(u["b_last"] + m, u["wmax"])
        decay = jnp.exp(u["b_last"] + m - m_new)
        gain = jnp.exp(u["wmax"] - m_new)
        ct_scr[u["unit"]] = (jnp.concatenate([decay, decay], axis=1) * ct
                             + jnp.concatenate([gain, gain], axis=1) * u["upd"])
        m_scr[u["unit"]] = m_new


def _mlstm(qb, kbt, vb, bc, w0, grows, stat, batch, seq, cps):
    n_tok = qb.shape[0]
    rows = cps * CHUNK_B
    ns = seq // rows
    fwd = lambda b, j: b * ns + j
    bwd = lambda b, j: b * ns + ns - 1 - j

    def specs(blk):
        return [pl.BlockSpec((rows, D_B), lambda b, j: (blk(b, j), 0)),
                pl.BlockSpec((D_B, rows), lambda b, j: (0, blk(b, j))),
                pl.BlockSpec((rows, D_B), lambda b, j: (blk(b, j), 0)),
                pl.BlockSpec((rows, LANES), lambda b, j: (blk(b, j), 0)),
                pl.BlockSpec((rows, LANES), lambda b, j: (blk(b, j), 0)),
                pl.BlockSpec((GATE_ROWS, rows), lambda b, j: (0, blk(b, j))),
                pl.BlockSpec((cps, SUBLANES, LANES), lambda b, j: (blk(b, j), 0, 0))]

    out = lambda blk: pl.BlockSpec((rows, D_B), lambda b, j: (blk(b, j), 0))
    n_units = 2 * N_HEADS_B
    args = (qb, kbt, vb, bc, w0, grows, stat)
    return pl.pallas_call(
        functools.partial(_mlstm_kernel, cps=cps),
        grid=(batch, ns),
        in_specs=specs(fwd) + specs(bwd),
        out_specs=(out(fwd), out(bwd)),
        out_shape=(jax.ShapeDtypeStruct((n_tok, D_B), BF16), jax.ShapeDtypeStruct((n_tok, D_B), BF16)),
        scratch_shapes=[pltpu.VMEM((n_units, HEAD_DIM_B, 2 * HEAD_DIM_B), F32),
                        pltpu.VMEM((n_units, 1, LANES), F32)],
        compiler_params=_params(2),
        name="mlstm",
    )(*args, *args)


def _out_proj_kernel(x_ref, oa_ref, hf_ref, hb_ref, ob_ref, hg_ref, w_ref, o_ref):
    hsum = hf_ref[...].astype(F32) + hb_ref[...].astype(F32)
    parts = []
    for hd in range(N_HEADS_B):
        hh = hsum[:, hd * HEAD_DIM_B:(hd + 1) * HEAD_DIM_B]
        parts.append(hh * lax.rsqrt(jnp.mean(hh * hh, axis=-1, keepdims=True) + EPS))
    hn = jnp.concatenate(parts, axis=1) * hg_ref[...]
    out_b = (hn * _sigmoid(ob_ref[...].astype(F32))).astype(BF16)
    o_ref[...] = x_ref[...] + _dot(oa_ref[...], w_ref[:D_A]) + _dot(out_b, w_ref[D_A:])


def _out_proj(x2d, out_a, hf, hb, ob, hg, w, tm):
    n_tok = x2d.shape[0]
    tok = lambda d: pl.BlockSpec((tm, d), lambda i: (i, 0))
    return pl.pallas_call(
        _out_proj_kernel,
        grid=(n_tok // tm,),
        in_specs=[tok(D_MODEL), tok(D_A), tok(D_B), tok(D_B), tok(D_B),
                  pl.BlockSpec((1, D_B), lambda i: (0, 0)),
                  pl.BlockSpec((D_A + D_B, D_MODEL), lambda i: (0, 0))],
        out_specs=tok(D_MODEL),
        out_shape=jax.ShapeDtypeStruct((n_tok, D_MODEL), F32),
        compiler_params=_params(1),
        name="out_proj",
    )(x2d, out_a, hf, hb, ob, hg, w)


MXU_COLS = 256


def _ffn_kernel(xp_ref, x_ref, xn_ref, gain_ref, wu_ref, cw_ref, cb_ref, wd_ref, o_ref, *, tiles_per_seq):
    i = pl.program_id(0)
    tm = x_ref.shape[0]
    first = (i % tiles_per_seq) == 0
    last = (i % tiles_per_seq) == tiles_per_seq - 1
    hext = _normed_window(xp_ref, x_ref, xn_ref, gain_ref[...], first, last)
    acc = x_ref[...]
    for lo, hi in ((0, 5 * MXU_COLS), (5 * MXU_COLS, D_FF)):
        gcols = slice(lo, hi)
        vcols = slice(D_FF + lo, D_FF + hi)
        gate = _conv3(_dot(hext, wu_ref[:, gcols]), cw_ref[:, gcols], tm) + cb_ref[:, gcols]
        val = _conv3(_dot(hext, wu_ref[:, vcols]), cw_ref[:, vcols], tm) + cb_ref[:, vcols]
        acc = acc + _dot((gate * _sigmoid(gate) * val).astype(BF16), wd_ref[lo:hi, :])
    o_ref[...] = acc


def _ffn(x2d, seq, tm, gain, w_up, cw, cb, w_down):
    n_tok = x2d.shape[0]
    prev, main, nxt = _halo_specs(tm, D_MODEL, n_tok)
    full = lambda r: pl.BlockSpec((r, 2 * D_FF), lambda i: (0, 0))
    once = pl.Buffered(1)
    return pl.pallas_call(
        functools.partial(_ffn_kernel, tiles_per_seq=seq // tm),
        grid=(n_tok // tm,),
        in_specs=[prev, main, nxt, pl.BlockSpec((1, D_MODEL), lambda i: (0, 0)),
                  pl.BlockSpec((D_MODEL, 2 * D_FF), lambda i: (0, 0), pipeline_mode=once), full(3), full(1),
                  pl.BlockSpec((D_FF, D_MODEL), lambda i: (0, 0), pipeline_mode=once)],
        out_specs=pl.BlockSpec((tm, D_MODEL), lambda i: (i, 0)),
        out_shape=jax.ShapeDtypeStruct((n_tok, D_MODEL), F32),
        compiler_params=_params(1),
        name="ffn",
    )(x2d, x2d, x2d, gain, w_up, cw, cb, w_down)


def _conf_glu_kernel(x_ref, gain_ref, w_ref, b_ref, o_ref):
    h = _rms(x_ref[...], gain_ref[...]).astype(BF16)
    u = _dot(h, w_ref[...]) + b_ref[...]
    o_ref[...] = u[:, :D_MODEL] * _sigmoid(u[:, D_MODEL:])


def _conf_glu(x2d, gain, w, b, tm):
    n_tok = x2d.shape[0]
    return pl.pallas_call(
        _conf_glu_kernel,
        grid=(n_tok // tm,),
        in_specs=[pl.BlockSpec((tm, D_MODEL), lambda i: (i, 0)), pl.BlockSpec((1, D_MODEL), lambda i: (0, 0)),
                  pl.BlockSpec((D_MODEL, 2 * D_MODEL), lambda i: (0, 0)),
                  pl.BlockSpec((1, 2 * D_MODEL), lambda i: (0, 0))],
        out_specs=pl.BlockSpec((tm, D_MODEL), lambda i: (i, 0)),
        out_shape=jax.ShapeDtypeStruct((n_tok, D_MODEL), F32),
        compiler_params=_params(1),
        name="conf_glu",
    )(x2d, gain, w, b)


CONV_ROWS = 64


def _conf_conv_kernel(up_ref, u_ref, un_ref, x_ref, wdw_ref, bdw_ref, lng_ref, lnb_ref, w2_ref, b2_ref, o_ref,
                      rot_scr, act_scr, *, tiles_per_seq):
    i = pl.program_id(0)
    tm = u_ref.shape[0]
    n = tm + 2 * HALO
    first = (i % tiles_per_seq) == 0
    last = (i % tiles_per_seq) == tiles_per_seq - 1
    win = jnp.concatenate([jnp.where(first, 0.0, up_ref[...]), u_ref[...], jnp.where(last, 0.0, un_ref[...])], axis=0)
    rot_scr[0] = win
    for r in range(1, 8):
        rot_scr[r] = pltpu.roll(win, n - r, 0)

    for c in range(D_MODEL // LANES):
        lanes = slice(c * LANES, (c + 1) * LANES)
        taps = [wdw_ref[k, :, lanes] for k in range(CONV_C)]

        def block(rb, carry, lanes=lanes, taps=taps):
            r0 = pl.multiple_of(rb * CONV_ROWS, CONV_ROWS)
            for a in range(CONV_ROWS // SUBLANES):
                acc = None
                for k in range(CONV_C):
                    shift = k + 1
                    rows = pl.ds(r0 + (shift // SUBLANES + a) * SUBLANES, SUBLANES)
                    term = taps[k] * rot_scr[shift % SUBLANES, rows, lanes]
                    acc = term if acc is None else acc + term
                act_scr[pl.ds(r0 + a * SUBLANES, SUBLANES), lanes] = acc
            return carry

        lax.fori_loop(0, tm // CONV_ROWS, block, 0)
    conv = act_scr[...] + bdw_ref[...]
    xc = conv - jnp.mean(conv, axis=-1, keepdims=True)
    y = xc * lax.rsqrt(jnp.mean(xc * xc, axis=-1, keepdims=True) + EPS) * lng_ref[...] + lnb_ref[...]
    act = (y * _sigmoid(y)).astype(BF16)
    o_ref[...] = x_ref[...] + _dot(act, w2_ref[...]) + b2_ref[...]


def _conf_conv(u, x2d, seq, tm, wdw, bdw, lng, lnb, w2, b2):
    n_tok = x2d.shape[0]
    prev, main, nxt = _halo_specs(tm, D_MODEL, n_tok)
    vec = pl.BlockSpec((1, D_MODEL), lambda i: (0, 0))
    return pl.pallas_call(
        functools.partial(_conf_conv_kernel, tiles_per_seq=seq // tm),
        grid=(n_tok // tm,),
        in_specs=[prev, main, nxt, pl.BlockSpec((tm, D_MODEL), lambda i: (i, 0)),
                  pl.BlockSpec((CONV_C, SUBLANES, D_MODEL), lambda i: (0, 0, 0)), vec, vec, vec,
                  pl.BlockSpec((D_MODEL, D_MODEL), lambda i: (0, 0)), vec],
        out_specs=pl.BlockSpec((tm, D_MODEL), lambda i: (i, 0)),
        out_shape=jax.ShapeDtypeStruct((n_tok, D_MODEL), F32),
        scratch_shapes=[pltpu.VMEM((8, tm + 2 * HALO, D_MODEL), F32), pltpu.VMEM((tm, D_MODEL), F32)],
        compiler_params=_params(1),
        name="conf_conv",
    )(u, u, u, x2d, wdw, bdw, lng, lnb, w2, b2)


def _rope_tables(seq):
    pos = jnp.arange(seq)
    inv = ROPE_THETA ** (-jnp.arange(ROPE_PAIRS, dtype=F32) / ROPE_PAIRS)
    lane = np.arange(HEAD_DIM_A)
    section, half, pair = lane // (2 * ROPE_PAIRS), (lane // ROPE_PAIRS) % 2, lane % ROPE_PAIRS
    row_idx = (pos // GRID_W).astype(F32)[:, None]
    col_idx = (pos % GRID_W).astype(F32)[:, None]
    ang = jnp.where(jnp.asarray(section == 0)[None, :], row_idx, col_idx) * inv[pair][None, :]
    sign = jnp.asarray(np.where(half == 0, -1.0, 1.0), F32)[None, :]
    return jnp.tile(jnp.cos(ang), (1, 2)), jnp.tile(jnp.sin(ang) * sign, (1, 2))


def _segment_mean_matrix(n, width):
    seg = np.arange(n) // width
    return jnp.asarray((seg[:, None] == seg[None, :]).astype(np.float32) / width, BF16)


def _tile_size(seq, want):
    return min(want, seq)


def _trunk(x, p):
    batch, seq, _ = x.shape
    n_tok = batch * seq
    x2d = x.reshape(n_tok, D_MODEL)
    tm = _tile_size(seq, 512)
    tw = 2 * tm if n_tok % (2 * tm) == 0 else tm
    cps = next(c for c in (8, 4, 2) if seq % (c * CHUNK_B) == 0)

    qt, k, vt, qb, kbt, vb, ob, g = _in_proj(x2d, seq, tm, p["mix_norm_e"], p["w_in"], p["cos"], p["sin"],
                                            p["segq"], p["segk"], p["qg"], p["kg"], p["w_qk_conv"], p["w_v_t"])
    out_a = _attention(qt, k, vt, batch, seq, _tile_size(seq, 512))
    bc, w0, grows, stat = _gate_prep(g, p["b_gates_row"], tm)
    hf, hb = _mlstm(qb, kbt, vb, bc, w0, grows, stat, batch, seq, cps)
    x2d = _out_proj(x2d, out_a, hf, hb, ob, p["h_gain"], p["w_out"], tw)
    x2d = _ffn(x2d, seq, tm, p["ffn_norm0"], p["w_up0"], p["w_dw_ff0"], p["b_dw_ff0"], p["w_down0"])
    u = _conf_glu(x2d, p["mix_norm_o"], p["w_pw1"], p["b_pw1"], tw)
    x2d = _conf_conv(u, x2d, seq, tm, p["w_dw_c"], p["b_dw_c"], p["ln_g"], p["ln_b"],
                     p["w_pw2"], p["b_pw2"])
    x2d = _ffn(x2d, seq, tm, p["ffn_norm1"], p["w_up1"], p["w_dw_ff1"], p["b_dw_ff1"], p["w_down1"])
    return x2d.reshape(batch, seq, D_MODEL)


def kernel(x_prompt, x_sample, mix_norm_e, w_in, q_gain_a, k_gain_a, w_qk_conv_b, b_gates_b, h_gain_b, w_out_e, mix_norm_o, w_pw1_c, b_pw1_c, w_dw_c, b_dw_c, ln_g_c, ln_b_c, w_pw2_c, b_pw2_c, ffn_norm, w_up, w_dw_ff, b_dw_ff, w_down):
    row = lambda a: a.reshape(1, -1).astype(F32)
    p = {
        "mix_norm_e": row(mix_norm_e[0]),
        "w_in": jnp.pad(w_in[0].astype(BF16), ((0, 0), (0, IN_COLS_PAD - IN_COLS))),
        "segq": _segment_mean_matrix(D_A, HEAD_DIM_A),
        "segk": _segment_mean_matrix(D_KV_A, HEAD_DIM_A),
        "qg": row(jnp.tile(q_gain_a[0], N_HEADS_A)),
        "kg": row(jnp.tile(k_gain_a[0], N_KV_HEADS_A)),
        "w_qk_conv": w_qk_conv_b[0].astype(F32),
        "w_v_t": w_in[0][:, _O_VA:_O_QKB].T.astype(BF16),
        "b_gates_row": jnp.pad(row(b_gates_b[0]), ((0, 0), (0, LANES - N_GATES_B))),
        "h_gain": row(h_gain_b[0]),
        "w_out": w_out_e[0].astype(BF16),
        "mix_norm_o": row(mix_norm_o[0]),
        "w_pw1": w_pw1_c[0].astype(BF16),
        "b_pw1": row(b_pw1_c[0]),
        "w_dw_c": jnp.broadcast_to(w_dw_c[0].astype(F32)[:, None, :], (CONV_C, SUBLANES, D_MODEL)),
        "b_dw_c": row(b_dw_c[0]),
        "ln_g": row(ln_g_c[0]),
        "ln_b": row(ln_b_c[0]),
        "w_pw2": w_pw2_c[0].astype(BF16),
        "b_pw2": row(b_pw2_c[0]),
    }
    for layer in range(2):
        p[f"ffn_norm{layer}"] = row(ffn_norm[layer])
        p[f"w_up{layer}"] = w_up[layer].astype(BF16)
        p[f"w_dw_ff{layer}"] = w_dw_ff[layer].astype(F32)
        p[f"b_dw_ff{layer}"] = row(b_dw_ff[layer])
        p[f"w_down{layer}"] = w_down[layer].astype(BF16)
    outs = []
    for x in (x_prompt, x_sample):
        p["cos"], p["sin"] = _rope_tables(x.shape[1])
        outs.append(_trunk(x, p))
    return tuple(outs)
```
